```python
import jax, jax.numpy as jnp
from jax import lax
import numpy as np

D_MODEL = 2048
BATCH = 2
SEQ = 16384
DEPTH = 4

N_A_LAYERS = DEPTH // 2
N_B_LAYERS = DEPTH - N_A_LAYERS
GLA_HEADS = 4
GLA_DK = D_MODEL // 2
GLA_DV = D_MODEL
GLA_HEAD_K = GLA_DK // GLA_HEADS
GLA_HEAD_V = GLA_DV // GLA_HEADS
GLA_GATE_RANK = 16
GLA_TAU = 16.0
GLA_CHUNK = 64
GLA_IN = 2 * GLA_DK + 2 * GLA_DV + GLA_GATE_RANK
MLA_HEADS = 16
MLA_Q_LORA = 512
MLA_KV_LORA = 512
MLA_NOPE = 128
MLA_ROPE = 64
MLA_V = 128
ROPE_THETA = 10000.0
Q_BLOCK = 128
D_FF = 4 * D_MODEL
RMS_EPS = 1e-6

kernel_name = 'yoco_gla_mla_hybrid'


def rmsnorm(x, g):
    xf = x.astype(jnp.float32)
    y = xf * lax.rsqrt(jnp.mean(xf * xf, axis=-1, keepdims=True) + RMS_EPS)
    return (y * g.astype(jnp.float32)).astype(x.dtype)


def rope(x, positions):
    r = x.shape[-1]
    half = r // 2
    freqs = ROPE_THETA ** (-jnp.arange(0, r, 2, dtype=jnp.float32) / r)
    ang = positions.astype(jnp.float32)[..., None] * freqs
    cos, sin = jnp.cos(ang)[:, :, None, :], jnp.sin(ang)[:, :, None, :]
    x1, x2 = x[..., :half].astype(jnp.float32), x[..., half:].astype(jnp.float32)
    return jnp.concatenate([x1 * cos - x2 * sin, x1 * sin + x2 * cos], axis=-1).astype(x.dtype)


def _to_chunks(t, n_heads):
    b, s, hd = t.shape
    return t.reshape(b, s // GLA_CHUNK, GLA_CHUNK, n_heads, hd // n_heads).transpose(1, 0, 3, 2, 4)


def gla_mix(xn, w_in, w_gate_up, b_gate, g_norm, w_out):
    bsz, seq, _ = xn.shape
    f32 = jnp.float32
    q, k, v, r, g_low = jnp.split(xn @ w_in, [GLA_DK, 2 * GLA_DK, 2 * GLA_DK + GLA_DV, 2 * GLA_DK + 2 * GLA_DV], axis=-1)
    log_a = jax.nn.log_sigmoid((g_low @ w_gate_up + b_gate).astype(f32)) / GLA_TAU
    qc = _to_chunks(q.astype(f32) * GLA_HEAD_K ** -0.5, GLA_HEADS)
    kc = _to_chunks(k.astype(f32), GLA_HEADS)
    vc = _to_chunks(v.astype(f32), GLA_HEADS)
    gc = _to_chunks(log_a, GLA_HEADS)
    causal = jnp.tril(jnp.ones((GLA_CHUNK, GLA_CHUNK), dtype=bool))

    def step(state, inp):
        qb, kb, vb, gb = inp
        b = jnp.cumsum(gb, axis=2)
        o_inter = jnp.einsum('bhcd,bhde->bhce', qb * jnp.exp(b), state)
        diff = b[:, :, :, None, :] - b[:, :, None, :, :]
        decay = jnp.exp(jnp.where(causal[:, :, None], diff, -jnp.inf))
        scores = jnp.einsum('bhid,bhjd,bhijd->bhij', qb, kb, decay)
        o_intra = jnp.einsum('bhij,bhje->bhie', scores, vb)
        b_last = b[:, :, -1:, :]
        state = state * jnp.exp(b_last)[:, :, 0, :, None] + jnp.einsum('bhcd,bhce->bhde', kb * jnp.exp(b_last - b), vb)
        return state, o_inter + o_intra

    state0 = jnp.zeros((bsz, GLA_HEADS, GLA_HEAD_K, GLA_HEAD_V), f32)
    _, o = lax.scan(step, state0, (qc, kc, vc, gc))
    o = o.transpose(1, 0, 3, 2, 4).reshape(bsz, seq, GLA_HEADS, GLA_HEAD_V)
    o = rmsnorm(o, g_norm).reshape(bsz, seq, GLA_DV).astype(xn.dtype)
    return (o * jax.nn.silu(r)) @ w_out


def mla_shared_kv(h, positions, norm_in, w_dkv, kv_norm, w_uk, w_uv):
    bsz, seq, _ = h.shape
    ckv = rmsnorm(h, norm_in) @ w_dkv
    c = rmsnorm(ckv[..., :MLA_KV_LORA], kv_norm)
    k_rope = rope(ckv[..., None, MLA_KV_LORA:], positions)[:, :, 0, :]
    k_nope = (c @ w_uk).reshape(bsz, seq, MLA_HEADS, MLA_NOPE)
    v = (c @ w_uv).reshape(bsz, seq, MLA_HEADS, MLA_V)
    return k_nope, k_rope, v


def mla_mix(xn, positions, w_dq, q_norm, w_uq, w_o, k_nope, k_rope, v):
    bsz, seq, _ = xn.shape
    cq = rmsnorm(xn @ w_dq, q_norm)
    q = (cq @ w_uq).reshape(bsz, seq, MLA_HEADS, MLA_NOPE + MLA_ROPE)
    q_nope = q[..., :MLA_NOPE]
    q_rope = rope(q[..., MLA_NOPE:], positions)
    nb = seq // Q_BLOCK
    scale = (MLA_NOPE + MLA_ROPE) ** -0.5
    key_idx = jnp.arange(seq)

    def blocks(t):
        return t.reshape(bsz, nb, Q_BLOCK, *t.shape[2:]).swapaxes(0, 1)

    def attend(inp):
        qn, qr, blk = inp
        s = jnp.einsum('bqhd,bkhd->bhqk', qn, k_nope) + jnp.einsum('bqhr,bkr->bhqk', qr, k_rope)
        q_idx = blk * Q_BLOCK + jnp.arange(Q_BLOCK)
        mask = key_idx[None, :] <= q_idx[:, None]
        p = jax.nn.softmax(jnp.where(mask, s.astype(jnp.float32) * scale, -jnp.inf), axis=-1).astype(v.dtype)
        return jnp.einsum('bhqk,bkhd->bqhd', p, v)

    o = lax.map(attend, (blocks(q_nope), blocks(q_rope), jnp.arange(nb)))
    o = o.swapaxes(0, 1).reshape(bsz, seq, MLA_HEADS * MLA_V)
    return o @ w_o


def sq_relu_mlp(xn, w1, w2):
    return jnp.square(jax.nn.relu(xn @ w1)) @ w2


def setup_inputs(seed: int = 0) -> dict:
    key = jax.random.key(seed)
    ks = jax.random.split(key, 24)
    f32 = jnp.float32

    def dense(k, shape, fan_in):
        return jax.random.normal(k, shape, f32) * fan_in ** -0.5

    def gain(k, shape):
        return 1.0 + 0.02 * jax.random.normal(k, shape, f32)

    return {
        'x': jax.random.normal(ks[0], (BATCH, SEQ, D_MODEL), f32),
        'positions': jnp.broadcast_to(jnp.arange(SEQ, dtype=jnp.int32), (BATCH, SEQ)),
        'norm_mix': gain(ks[1], (DEPTH, D_MODEL)),
        'norm_mlp': gain(ks[2], (DEPTH, D_MODEL)),
        'gla_w_in': dense(ks[3], (N_A_LAYERS, D_MODEL, GLA_IN), D_MODEL),
        'gla_w_gate_up': dense(ks[4], (N_A_LAYERS, GLA_GATE_RANK, GLA_DK), GLA_GATE_RANK),
        'gla_b_gate': 0.01 * jax.random.normal(ks[5], (N_A_LAYERS, GLA_DK), f32),
        'gla_norm': gain(ks[6], (N_A_LAYERS, GLA_HEAD_V)),
        'gla_w_out': dense(ks[7], (N_A_LAYERS, GLA_DV, D_MODEL), GLA_DV),
        'kv_norm_in': gain(ks[8], (D_MODEL,)),
        'mla_w_dkv': dense(ks[9], (D_MODEL, MLA_KV_LORA + MLA_ROPE), D_MODEL),
        'mla_kv_norm': gain(ks[10], (MLA_KV_LORA,)),
        'mla_w_uk': dense(ks[11], (MLA_KV_LORA, MLA_HEADS * MLA_NOPE), MLA_KV_LORA),
        'mla_w_uv': dense(ks[12], (MLA_KV_LORA, MLA_HEADS * MLA_V), MLA_KV_LORA),
        'mla_w_dq': dense(ks[13], (N_B_LAYERS, D_MODEL, MLA_Q_LORA), D_MODEL),
        'mla_q_norm': gain(ks[14], (N_B_LAYERS, MLA_Q_LORA)),
        'mla_w_uq': dense(ks[15], (N_B_LAYERS, MLA_Q_LORA, MLA_HEADS * (MLA_NOPE + MLA_ROPE)), MLA_Q_LORA),
        'mla_w_o': dense(ks[16], (N_B_LAYERS, MLA_HEADS * MLA_V, D_MODEL), MLA_HEADS * MLA_V),
        'mlp_w1': dense(ks[17], (DEPTH, D_MODEL, D_FF), D_MODEL),
        'mlp_w2': dense(ks[18], (DEPTH, D_FF, D_MODEL), D_FF),
        'final_norm': gain(ks[19], (D_MODEL,)),
    }


def reference(x, positions, norm_mix, norm_mlp, gla_w_in, gla_w_gate_up, gla_b_gate, gla_norm, gla_w_out,
              kv_norm_in, mla_w_dkv, mla_kv_norm, mla_w_uk, mla_w_uv,
              mla_w_dq, mla_q_norm, mla_w_uq, mla_w_o, mlp_w1, mlp_w2, final_norm):
    h = x
    shared = None
    for layer in range(DEPTH):
        if layer < N_A_LAYERS:
            a = layer
            h = h + gla_mix(rmsnorm(h, norm_mix[layer]), gla_w_in[a], gla_w_gate_up[a], gla_b_gate[a],
                            gla_norm[a], gla_w_out[a])
        else:
            if layer == N_A_LAYERS:
                shared = mla_shared_kv(h, positions, kv_norm_in, mla_w_dkv, mla_kv_norm, mla_w_uk, mla_w_uv)
            b = layer - N_A_LAYERS
            k_nope, k_rope, v = shared
            h = h + mla_mix(rmsnorm(h, norm_mix[layer]), positions, mla_w_dq[b], mla_q_norm[b], mla_w_uq[b],
                            mla_w_o[b], k_nope, k_rope, v)
        h = h + sq_relu_mlp(rmsnorm(h, norm_mlp[layer]), mlp_w1[layer], mlp_w2[layer])
    return rmsnorm(h, final_norm)
```

```python
import functools

import jax
import jax.numpy as jnp
from jax import lax
from jax.experimental import pallas as pl
from jax.experimental.pallas import tpu as pltpu

BF16 = jnp.bfloat16
F32 = jnp.float32

RMS_EPS = 1e-6
ROPE_THETA = 10000.0
GLA_TAU = 16.0
GLA_CHUNK = 64
GLA_SUB = 16
LANES = 128
NEG_BIG = -1e30

VMEM_LIMIT_BYTES = 56 * 1024 * 1024

TILES = dict(
    inproj_m=512, inproj_n=1024,
    gla_tokens=512,
    outproj_m=512,
    mlp_m=512, mlp_f=1024,
    kv_m=512,
    q_m=512,
    rope_m=2048,
    attn_q=1024, attn_k=512, attn_heads=4,
)


def _cparams(sem):
    return pltpu.CompilerParams(dimension_semantics=sem, vmem_limit_bytes=VMEM_LIMIT_BYTES)


def _tile(n, t):
    t = min(n, t)
    assert n % t == 0, (n, t)
    return t


def _rms(x, g):
    return x * lax.rsqrt(jnp.mean(x * x, axis=-1, keepdims=True) + RMS_EPS) * g


def _dot(a, b):
    return jnp.dot(a, b, preferred_element_type=F32)


def _dot_nt(a, b):
    return lax.dot_general(a, b, (((1,), (1,)), ((), ())), preferred_element_type=F32)


def _dot_tn(a, b):
    return lax.dot_general(a, b, (((0,), (0,)), ((), ())), preferred_element_type=F32)


def _gla_inproj_kernel(x_ref, g_ref, w_ref, wg_ref, wup_ref, bg_ref, out_ref, la_ref, xn_ref):
    @pl.when(pl.program_id(1) == 0)
    def _():
        xn = _rms(x_ref[...], g_ref[...]).astype(BF16)
        xn_ref[...] = xn
        g_low = _dot(xn, wg_ref[...])
        z = _dot(g_low.astype(BF16), wup_ref[...]) + bg_ref[...]
        log_sig = jnp.minimum(z, 0.0) - jnp.log1p(jnp.exp(-jnp.abs(z)))
        la_ref[...] = log_sig * (1.0 / GLA_TAU)

    out_ref[...] = _dot(xn_ref[...], w_ref[...]).astype(out_ref.dtype)


def _gla_inproj(h, g, w, wg, wup, bg):
    t, d = h.shape
    n = w.shape[1]
    dk = wup.shape[1]
    tm = _tile(t, TILES["inproj_m"])
    tn = _tile(n, TILES["inproj_n"])
    return pl.pallas_call(
        _gla_inproj_kernel,
        grid=(t // tm, n // tn),
        in_specs=[
            pl.BlockSpec((tm, d), lambda i, j: (i, 0)),
            pl.BlockSpec((1, d), lambda i, j: (0, 0)),
            pl.BlockSpec((d, tn), lambda i, j: (0, j)),
            pl.BlockSpec(wg.shape, lambda i, j: (0, 0)),
            pl.BlockSpec(wup.shape, lambda i, j: (0, 0)),
            pl.BlockSpec((1, dk), lambda i, j: (0, 0)),
        ],
        out_specs=[
            pl.BlockSpec((tm, tn), lambda i, j: (i, j)),
            pl.BlockSpec((tm, dk), lambda i, j: (i, 0)),
        ],
        out_shape=[
            jax.ShapeDtypeStruct((t, n), BF16),
            jax.ShapeDtypeStruct((t, dk), F32),
        ],
        scratch_shapes=[pltpu.VMEM((tm, d), BF16)],
        compiler_params=_cparams(("parallel", "arbitrary")),
        name="gla_inproj",
    )(h, g, w, wg, wup, bg)


def _gla_kernel(q_ref, k_ref, v_ref, r_ref, la_ref, gn_ref, out_ref, st_ref, *, n_chunks, scale):
    c_len, sub = GLA_CHUNK, GLA_SUB
    n_sub = c_len // sub

    @pl.when(pl.program_id(2) == 0)
    def _():
        st_ref[...] = jnp.zeros_like(st_ref)

    row = lax.broadcasted_iota(jnp.int32, (c_len, c_len), 0)
    col = lax.broadcasted_iota(jnp.int32, (c_len, c_len), 1)
    tril = (col <= row).astype(F32)
    srow = lax.broadcasted_iota(jnp.int32, (sub * sub, c_len), 0)
    scol = lax.broadcasted_iota(jnp.int32, (sub * sub, c_len), 1)
    sub_col = lax.broadcasted_iota(jnp.int32, (sub, c_len), 1)
    gn = gn_ref[...]

    def chunk(c, carry):
        r0 = pl.multiple_of(c * c_len, c_len)
        g = la_ref[pl.ds(r0, c_len), :]
        b = jnp.dot(tril, g, precision=lax.Precision.HIGHEST,
                    preferred_element_type=F32)
        q = q_ref[pl.ds(r0, c_len), :].astype(F32) * scale
        k = k_ref[pl.ds(r0, c_len), :].astype(F32)
        v = v_ref[pl.ds(r0, c_len), :]
        b_last = b[c_len - 1:c_len, :]

        st = st_ref[...]
        o = _dot_nt((q * jnp.exp(b)).astype(BF16), st.astype(BF16))
        k_dec = (k * jnp.exp(b_last - b)).astype(BF16)
        st_ref[...] = st * jnp.exp(b_last) + _dot_tn(v, k_dec)

        blocks = []
        for s in range(n_sub):
            lo = s * sub
            b_s = b[lo:lo + sub, :]
            q_s = q[lo:lo + sub, :]
            stacked = []
            for jj in range(sub):
                b_j = b[lo + jj:lo + jj + 1, :]
                stacked.append((q_s * jnp.exp(jnp.minimum(b_s - b_j, 0.0))).astype(BF16))
            stacked = jnp.concatenate(stacked, axis=0)
            full = _dot_nt(stacked, k.astype(BF16))
            full = jnp.where(scol == lo + srow // sub, full, 0.0)
            diag = jnp.sum(full.reshape(sub, sub, c_len), axis=0)
            if s == 0:
                blocks.append(diag)
            else:
                b_ref = b[lo:lo + 1, :]
                q_t = (q_s * jnp.exp(b_s - b_ref)).astype(BF16)
                k_t = (k * jnp.exp(jnp.minimum(b_ref - b, 0.0))).astype(BF16)
                off = _dot_nt(q_t, k_t)
                blocks.append(jnp.where(sub_col < lo, off, diag))
        scores = jnp.concatenate(blocks, axis=0)
        scores = jnp.where(col <= row, scores, 0.0)
        o = o + _dot(scores.astype(BF16), v)

        o = _rms(o, gn)
        r = r_ref[pl.ds(r0, c_len), :].astype(F32)
        out_ref[pl.ds(r0, c_len), :] = (o * (r * jax.nn.sigmoid(r))).astype(out_ref.dtype)
        return carry

    lax.fori_loop(0, n_chunks, chunk, 0)


def _gla(qkvr, log_a, gnorm, *, batch, seq, heads, dk_h, dv_h):
    t = qkvr.shape[0]
    tt = _tile(seq, TILES["gla_tokens"])
    ns = seq // tt
    nq = (heads * dk_h) // dk_h
    v_off = (2 * heads * dk_h) // dv_h
    r_off = v_off + heads

    def rows(b, h, i):
        return b * ns + i

    kern = functools.partial(_gla_kernel, n_chunks=tt // GLA_CHUNK, scale=float(dk_h) ** -0.5)
    return pl.pallas_call(
        kern,
        grid=(batch, heads, ns),
        in_specs=[
            pl.BlockSpec((tt, dk_h), lambda b, h, i: (rows(b, h, i), h)),
            pl.BlockSpec((tt, dk_h), lambda b, h, i: (rows(b, h, i), nq + h)),
            pl.BlockSpec((tt, dv_h), lambda b, h, i: (rows(b, h, i), v_off + h)),
            pl.BlockSpec((tt, dv_h), lambda b, h, i: (rows(b, h, i), r_off + h)),
            pl.BlockSpec((tt, dk_h), lambda b, h, i: (rows(b, h, i), h)),
            pl.BlockSpec((1, dv_h), lambda b, h, i: (0, 0)),
        ],
        out_specs=pl.BlockSpec((tt, dv_h), lambda b, h, i: (rows(b, h, i), h)),
        out_shape=jax.ShapeDtypeStruct((t, heads * dv_h), BF16),
        scratch_shapes=[pltpu.VMEM((dv_h, dk_h), F32)],
        compiler_params=_cparams(("parallel", "parallel", "arbitrary")),
        name="gla_chunk",
    )(qkvr, qkvr, qkvr, qkvr, log_a, gnorm)


def _outproj_kernel(h_ref, a_ref, w_ref, out_ref):
    out_ref[...] = h_ref[...] + _dot(a_ref[...], w_ref[...])


def _outproj(h, a, w):
    t, d = h.shape
    k = a.shape[1]
    tm = _tile(t, TILES["outproj_m"])
    return pl.pallas_call(
        _outproj_kernel,
        grid=(t // tm,),
        in_specs=[
            pl.BlockSpec((tm, d), lambda i: (i, 0)),
            pl.BlockSpec((tm, k), lambda i: (i, 0)),
            pl.BlockSpec((k, d), lambda i: (0, 0)),
        ],
        out_specs=pl.BlockSpec((tm, d), lambda i: (i, 0)),
        out_shape=jax.ShapeDtypeStruct((t, d), F32),
        compiler_params=_cparams(("parallel",)),
        name="outproj",
    )(h, a, w)


def _mlp_kernel(x_ref, g_ref, w1_ref, w2_ref, gf_ref, out_ref, xn_ref, *, final_norm):
    j = pl.program_id(1)

    @pl.when(j == 0)
    def _():
        x = x_ref[...]
        xn_ref[...] = _rms(x, g_ref[...]).astype(BF16)
        out_ref[...] = x

    hid = _dot(xn_ref[...], w1_ref[...])
    hid = jnp.square(jnp.maximum(hid, 0.0)).astype(BF16)
    out_ref[...] += _dot(hid, w2_ref[...])

    if final_norm:
        @pl.when(j == pl.num_programs(1) - 1)
        def _():
            out_ref[...] = _rms(out_ref[...], gf_ref[...])


def _mlp(h, g, w1, w2, gf, *, final_norm):
    t, d = h.shape
    f = w1.shape[1]
    tm = _tile(t, TILES["mlp_m"])
    tf = _tile(f, TILES["mlp_f"])
    return pl.pallas_call(
        functools.partial(_mlp_kernel, final_norm=final_norm),
        grid=(t // tm, f // tf),
        in_specs=[
            pl.BlockSpec((tm, d), lambda i, j: (i, 0)),
            pl.BlockSpec((1, d), lambda i, j: (0, 0)),
            pl.BlockSpec((d, tf), lambda i, j: (0, j)),
            pl.BlockSpec((tf, d), lambda i, j: (j, 0)),
            pl.BlockSpec((1, d), lambda i, j: (0, 0)),
        ],
        out_specs=pl.BlockSpec((tm, d), lambda i, j: (i, 0)),
        out_shape=jax.ShapeDtypeStruct((t, d), F32),
        scratch_shapes=[pltpu.VMEM((tm, d), BF16)],
        compiler_params=_cparams(("parallel", "arbitrary")),
        name="mlp",
    )(h, g, w1, w2, gf)


def _rope_table_kernel(pos_ref, freq_ref, sign_ref, keep_ref, cos_ref, sin_ref):
    ang = pos_ref[...].astype(F32) * freq_ref[...]
    cos_ref[...] = jnp.cos(ang) * keep_ref[...]
    sin_ref[...] = jnp.sin(ang) * sign_ref[...]


def _rope_tables(pos, rope_dim):
    t = pos.shape[0]
    half = rope_dim // 2
    freqs = ROPE_THETA ** (-jnp.arange(0, rope_dim, 2, dtype=F32) / rope_dim)
    pad = jnp.zeros((LANES - rope_dim,), F32)
    freq = jnp.concatenate([freqs, freqs, pad]).reshape(1, LANES)
    sign = jnp.concatenate([-jnp.ones((half,), F32), jnp.ones((half,), F32), pad]).reshape(1, LANES)
    keep = jnp.concatenate([jnp.ones((rope_dim,), F32), pad]).reshape(1, LANES)
    tm = _tile(t, TILES["rope_m"])
    row = pl.BlockSpec((1, LANES), lambda i: (0, 0))
    tab = pl.BlockSpec((tm, LANES), lambda i: (i, 0))
    return pl.pallas_call(
        _rope_table_kernel,
        grid=(t // tm,),
        in_specs=[pl.BlockSpec((tm, 1), lambda i: (i, 0)), row, row, row],
        out_specs=[tab, tab],
        out_shape=[jax.ShapeDtypeStruct((t, LANES), F32)] * 2,
        compiler_params=_cparams(("parallel",)),
        name="rope_tables",
    )(pos, freq, sign, keep)


def _mla_kv_kernel(x_ref, g_ref, wd_ref, gkv_ref, wuk_ref, wuv_ref, cos_ref, sin_ref,
                   k_ref, v_ref, *, heads, lora):
    xn = _rms(x_ref[...], g_ref[...]).astype(BF16)
    ckv = _dot(xn, wd_ref[...])
    c = _rms(ckv[:, :lora], gkv_ref[...]).astype(BF16)
    k_rope = ckv[:, lora:lora + LANES] * cos_ref[...] + ckv[:, lora + LANES:] * sin_ref[...]
    k_rope = k_rope.astype(BF16)
    k_nope = _dot(c, wuk_ref[...]).astype(BF16)
    v_ref[...] = _dot(c, wuv_ref[...]).astype(BF16)
    for h in range(heads):
        k_ref[:, 2 * h * LANES:(2 * h + 1) * LANES] = k_nope[:, h * LANES:(h + 1) * LANES]
        k_ref[:, (2 * h + 1) * LANES:(2 * h + 2) * LANES] = k_rope


def _mla_kv(h, g, wd, gkv, wuk, wuv, cos_t, sin_t, *, heads):
    t, d = h.shape
    lora = gkv.shape[1]
    tm = _tile(t, TILES["kv_m"])
    full = lambda a: pl.BlockSpec(a.shape, lambda i: (0, 0))
    return pl.pallas_call(
        functools.partial(_mla_kv_kernel, heads=heads, lora=lora),
        grid=(t // tm,),
        in_specs=[
            pl.BlockSpec((tm, d), lambda i: (i, 0)),
            full(g), full(wd), full(gkv), full(wuk), full(wuv),
            pl.BlockSpec((tm, LANES), lambda i: (i, 0)),
            pl.BlockSpec((tm, LANES), lambda i: (i, 0)),
        ],
        out_specs=[
            pl.BlockSpec((tm, 2 * heads * LANES), lambda i: (i, 0)),
            pl.BlockSpec((tm, heads * LANES), lambda i: (i, 0)),
        ],
        out_shape=[
            jax.ShapeDtypeStruct((t, 2 * heads * LANES), BF16),
            jax.ShapeDtypeStruct((t, heads * LANES), BF16),
        ],
        compiler_params=_cparams(("parallel",)),
        name="mla_kv",
    )(h, g, wd, gkv, wuk, wuv, cos_t, sin_t)


def _mla_q_kernel(x_ref, g_ref, wd_ref, gq_ref, wn_ref, wr_ref, ws_ref, cos_ref, sin_ref,
                  q_ref, *, heads, scale):
    xn = _rms(x_ref[...], g_ref[...]).astype(BF16)
    cq = _rms(_dot(xn, wd_ref[...]), gq_ref[...]).astype(BF16)
    q_nope = _dot(cq, wn_ref[...])
    q_rope = _dot(cq, wr_ref[...])
    q_swap = _dot(cq, ws_ref[...])
    cos_t = cos_ref[...]
    sin_t = sin_ref[...]
    for h in range(heads):
        sl = slice(h * LANES, (h + 1) * LANES)
        q_ref[:, 2 * h * LANES:(2 * h + 1) * LANES] = (q_nope[:, sl] * scale).astype(BF16)
        roped = q_rope[:, sl] * cos_t + q_swap[:, sl] * sin_t
        q_ref[:, (2 * h + 1) * LANES:(2 * h + 2) * LANES] = (roped * scale).astype(BF16)


def _mla_q(h, g, wd, gq, wn, wr, ws, cos_t, sin_t, *, heads, scale):
    t, d = h.shape
    tm = _tile(t, TILES["q_m"])
    full = lambda a: pl.BlockSpec(a.shape, lambda i: (0, 0))
    return pl.pallas_call(
        functools.partial(_mla_q_kernel, heads=heads, scale=scale),
        grid=(t // tm,),
        in_specs=[
            pl.BlockSpec((tm, d), lambda i: (i, 0)),
            full(g), full(wd), full(gq), full(wn), full(wr), full(ws),
            pl.BlockSpec((tm, LANES), lambda i: (i, 0)),
            pl.BlockSpec((tm, LANES), lambda i: (i, 0)),
        ],
        out_specs=pl.BlockSpec((tm, 2 * heads * LANES), lambda i: (i, 0)),
        out_shape=jax.ShapeDtypeStruct((t, 2 * heads * LANES), BF16),
        compiler_params=_cparams(("parallel",)),
        name="mla_q",
    )(h, g, wd, gq, wn, wr, ws, cos_t, sin_t)


def _attn_kernel(qi_ref, kj_ref, q_ref, k_ref, v_ref, out_ref, m_ref, l_ref, acc_ref,
                 *, group, tq, tk):
    p = pl.program_id(2)
    qi = qi_ref[p]
    kj = kj_ref[p]
    q0 = qi * tq
    k0 = kj * tk

    @pl.when(kj == 0)
    def _():
        m_ref[...] = jnp.full_like(m_ref, NEG_BIG)
        l_ref[...] = jnp.zeros_like(l_ref)
        acc_ref[...] = jnp.zeros_like(acc_ref)

    def step(masked):
        if masked:
            row = q0 + lax.broadcasted_iota(jnp.int32, (tq, tk), 0)
            col = k0 + lax.broadcasted_iota(jnp.int32, (tq, tk), 1)
            keep = col <= row
        for g in range(group):
            q = q_ref[:, 2 * g * LANES:(2 * g + 2) * LANES]
            k = k_ref[:, 2 * g * LANES:(2 * g + 2) * LANES]
            s = _dot_nt(q, k)
            if masked:
                s = jnp.where(keep, s, NEG_BIG)
            m_prev = m_ref[g]
            m_new = jnp.maximum(m_prev, jnp.max(s, axis=1, keepdims=True))
            alpha = jnp.exp(m_prev - m_new)
            pr = jnp.exp(s - m_new[:, :1])
            l_ref[g] = alpha * l_ref[g] + jnp.sum(pr, axis=1, keepdims=True)
            acc_ref[g] = alpha * acc_ref[g] + _dot(pr.astype(BF16), v_ref[:, g * LANES:(g + 1) * LANES])
            m_ref[g] = m_new

    crosses = k0 + tk - 1 > q0

    @pl.when(crosses)
    def _():
        step(True)

    @pl.when(jnp.logical_not(crosses))
    def _():
        step(False)

    @pl.when(k0 + tk >= q0 + tq)
    def _():
        for g in range(group):
            out_ref[:, g * LANES:(g + 1) * LANES] = (acc_ref[g] / l_ref[g]).astype(out_ref.dtype)


def _attention(q, k, v, *, batch, seq, heads):
    t = q.shape[0]
    tq = _tile(seq, TILES["attn_q"])
    tk = _tile(tq, TILES["attn_k"])
    group = _tile(heads, TILES["attn_heads"])
    nq, nk = seq // tq, seq // tk
    pairs = [(i, j) for i in range(nq) for j in range(((i + 1) * tq - 1) // tk + 1)]
    qi_tab = jnp.asarray([a for a, _ in pairs], jnp.int32)
    kj_tab = jnp.asarray([b for _, b in pairs], jnp.int32)

    grid_spec = pltpu.PrefetchScalarGridSpec(
        num_scalar_prefetch=2,
        grid=(batch, heads // group, len(pairs)),
        in_specs=[
            pl.BlockSpec((tq, 2 * group * LANES), lambda b, h, p, qi, kj: (b * nq + qi[p], h)),
            pl.BlockSpec((tk, 2 * group * LANES), lambda b, h, p, qi, kj: (b * nk + kj[p], h)),
            pl.BlockSpec((tk, group * LANES), lambda b, h, p, qi, kj: (b * nk + kj[p], h)),
        ],
        out_specs=pl.BlockSpec((tq, group * LANES), lambda b, h, p, qi, kj: (b * nq + qi[p], h)),
        scratch_shapes=[pltpu.VMEM((group, tq, LANES), F32)] * 3,
    )
    return pl.pallas_call(
        functools.partial(_attn_kernel, group=group, tq=tq, tk=tk),
        grid_spec=grid_spec,
        out_shape=jax.ShapeDtypeStruct((t, heads * LANES), BF16),
        compiler_params=_cparams(("parallel", "parallel", "arbitrary")),
        name="mla_attention",
    )(qi_tab, kj_tab, q, k, v)


def _pad_cols(w, n):
    return jnp.pad(w, ((0, 0), (0, n - w.shape[1])))


def _swap_halves(w):
    half = w.shape[-1] // 2
    return jnp.concatenate([w[..., half:], w[..., :half]], axis=-1)


def _pad_heads(w, heads):
    kdim = w.shape[0]
    w = w.reshape(kdim, heads, -1)
    w = jnp.pad(w, ((0, 0), (0, 0), (0, LANES - w.shape[-1])))
    return w.reshape(kdim, heads * LANES)


def kernel(x, positions, norm_mix, norm_mlp, gla_w_in, gla_w_gate_up, gla_b_gate, gla_norm, gla_w_out,
           kv_norm_in, mla_w_dkv, mla_kv_norm, mla_w_uk, mla_w_uv, mla_w_dq, mla_q_norm, mla_w_uq,
           mla_w_o, mlp_w1, mlp_w2, final_norm):
    batch, seq, d = x.shape
    t = batch * seq
    depth = norm_mix.shape[0]
    n_gla = gla_w_in.shape[0]

    gla_dk = gla_w_gate_up.shape[2]
    gla_dv = gla_w_out.shape[1]
    gla_dv_h = gla_norm.shape[1]
    gla_heads = gla_dv // gla_dv_h
    gla_dk_h = gla_dk // gla_heads
    n_main = 2 * gla_dk + 2 * gla_dv

    kv_lora = mla_kv_norm.shape[0]
    rope_dim = mla_w_dkv.shape[1] - kv_lora
    mla_heads = (mla_w_uq.shape[2] - mla_w_uk.shape[1]) // rope_dim
    nope_dim = mla_w_uk.shape[1] // mla_heads
    assert nope_dim == LANES and mla_w_uv.shape[1] == mla_heads * LANES and rope_dim <= LANES
    assert gla_dk_h % LANES == 0 and gla_dv_h % LANES == 0
    attn_scale = float(nope_dim + rope_dim) ** -0.5

    h = x.reshape(t, d)
    pos = positions.reshape(t, 1)
    row = lambda a: a.reshape(1, -1)

    cos_t = sin_t = k_cat = v_all = None
    for layer in range(depth):
        if layer < n_gla:
            a = layer
            w_in = gla_w_in[a]
            w_main = w_in[:, :n_main].astype(BF16)
            w_g = _pad_cols(w_in[:, n_main:], LANES).astype(BF16)
            w_up = jnp.pad(gla_w_gate_up[a], ((0, LANES - gla_w_gate_up.shape[1]), (0, 0))).astype(BF16)
            qkvr, log_a = _gla_inproj(h, row(norm_mix[layer]), w_main, w_g, w_up, row(gla_b_gate[a]))
            o = _gla(qkvr, log_a, row(gla_norm[a]), batch=batch, seq=seq, heads=gla_heads,
                     dk_h=gla_dk_h, dv_h=gla_dv_h)
            h = _outproj(h, o, gla_w_out[a].astype(BF16))
        else:
            b = layer - n_gla
            if layer == n_gla:
                cos_t, sin_t = _rope_tables(pos, rope_dim)
                w_lat = mla_w_dkv[:, :kv_lora]
                w_rope = mla_w_dkv[:, kv_lora:]
                w_dkv = jnp.concatenate(
                    [w_lat, _pad_cols(w_rope, LANES), _pad_cols(_swap_halves(w_rope), LANES)], axis=1)
                k_cat, v_all = _mla_kv(h, row(kv_norm_in), w_dkv.astype(BF16), row(mla_kv_norm),
                                       mla_w_uk.astype(BF16), mla_w_uv.astype(BF16), cos_t, sin_t,
                                       heads=mla_heads)
            w_uq = mla_w_uq[b].reshape(-1, mla_heads, nope_dim + rope_dim)
            q_lora = w_uq.shape[0]
            w_n = w_uq[:, :, :nope_dim].reshape(q_lora, mla_heads * nope_dim)
            w_r = w_uq[:, :, nope_dim:]
            w_rp = _pad_heads(w_r.reshape(q_lora, -1), mla_heads)
            w_sp = _pad_heads(_swap_halves(w_r).reshape(q_lora, -1), mla_heads)
            q_cat = _mla_q(h, row(norm_mix[layer]), mla_w_dq[b].astype(BF16), row(mla_q_norm[b]),
                           w_n.astype(BF16), w_rp.astype(BF16), w_sp.astype(BF16), cos_t, sin_t,
                           heads=mla_heads, scale=attn_scale)
            o = _attention(q_cat, k_cat, v_all, batch=batch, seq=seq, heads=mla_heads)
            h = _outproj(h, o, mla_w_o[b].astype(BF16))
        h = _mlp(h, row(norm_mlp[layer]), mlp_w1[layer].astype(BF16), mlp_w2[layer].astype(BF16),
                 row(final_norm), final_norm=(layer == depth - 1))
    return h.reshape(batch, seq, d)
```

```python
import functools

import jax
import jax.numpy as jnp
from jax import lax
from jax.experimental import pallas as pl
from jax.experimental.pallas import tpu as pltpu

BF16 = jnp.bfloat16
F32 = jnp.float32

RMS_EPS = 1e-6
ROPE_THETA = 10000.0
GLA_TAU = 16.0
GLA_CHUNK = 64
GLA_SUB = 16
LANES = 128
NEG_BIG = -1e30
LOG2_E = 1.4426950408889634

VMEM_LIMIT_BYTES = 56 * 1024 * 1024

TILES = dict(
    inproj_m=512, inproj_n=1024,
    gla_tokens=512,
    outproj_m=512,
    mlp_m=512, mlp_f=1024,
    kv_m=512,
    q_m=512,
    rope_m=2048,
    attn_q=1024, attn_k=512, attn_heads=4,
)


def _cparams(sem):
    return pltpu.CompilerParams(dimension_semantics=sem, vmem_limit_bytes=VMEM_LIMIT_BYTES)


def _tile(n, t):
    t = min(n, t)
    assert n % t == 0, (n, t)
    return t


def _rms(x, g):
    return x * lax.rsqrt(jnp.mean(x * x, axis=-1, keepdims=True) + RMS_EPS) * g


def _dot(a, b):
    return jnp.dot(a, b, preferred_element_type=F32)


def _dot_nt(a, b):
    return lax.dot_general(a, b, (((1,), (1,)), ((), ())), preferred_element_type=F32)


def _dot_tn(a, b):
    return lax.dot_general(a, b, (((0,), (0,)), ((), ())), preferred_element_type=F32)


def _gla_inproj_kernel(x_ref, g_ref, w_ref, wg_ref, wup_ref, bg_ref, out_ref, la_ref, xn_ref):
    @pl.when(pl.program_id(1) == 0)
    def _():
        xn = _rms(x_ref[...], g_ref[...]).astype(BF16)
        xn_ref[...] = xn
        g_low = _dot(xn, wg_ref[...])
        z = _dot(g_low.astype(BF16), wup_ref[...]) + bg_ref[...]
        log_sig = jnp.minimum(z, 0.0) - jnp.log1p(jnp.exp(-jnp.abs(z)))
        la_ref[...] = log_sig * (1.0 / GLA_TAU)

    out_ref[...] = _dot(xn_ref[...], w_ref[...]).astype(out_ref.dtype)


def _gla_inproj(h, g, w, wg, wup, bg):
    t, d = h.shape
    n = w.shape[1]
    dk = wup.shape[1]
    tm = _tile(t, TILES["inproj_m"])
    tn = _tile(n, TILES["inproj_n"])
    return pl.pallas_call(
        _gla_inproj_kernel,
        grid=(t // tm, n // tn),
        in_specs=[
            pl.BlockSpec((tm, d), lambda i, j: (i, 0)),
            pl.BlockSpec((1, d), lambda i, j: (0, 0)),
            pl.BlockSpec((d, tn), lambda i, j: (0, j)),
            pl.BlockSpec(wg.shape, lambda i, j: (0, 0)),
            pl.BlockSpec(wup.shape, lambda i, j: (0, 0)),
            pl.BlockSpec((1, dk), lambda i, j: (0, 0)),
        ],
        out_specs=[
            pl.BlockSpec((tm, tn), lambda i, j: (i, j)),
            pl.BlockSpec((tm, dk), lambda i, j: (i, 0)),
        ],
        out_shape=[
            jax.ShapeDtypeStruct((t, n), BF16),
            jax.ShapeDtypeStruct((t, dk), F32),
        ],
        scratch_shapes=[pltpu.VMEM((tm, d), BF16)],
        compiler_params=_cparams(("parallel", "arbitrary")),
        name="gla_inproj",
    )(h, g, w, wg, wup, bg)


def _gla_kernel(q_ref, k_ref, v_ref, r_ref, la_ref, gn_ref, out_ref, st_ref, *, n_chunks, scale):
    c_len, sub = GLA_CHUNK, GLA_SUB
    n_sub = c_len // sub

    @pl.when(pl.program_id(2) == 0)
    def _():
        st_ref[...] = jnp.zeros_like(st_ref)

    row = lax.broadcasted_iota(jnp.int32, (c_len, c_len), 0)
    col = lax.broadcasted_iota(jnp.int32, (c_len, c_len), 1)
    tril = (col <= row).astype(F32)
    srow = lax.broadcasted_iota(jnp.int32, (sub * sub, c_len), 0)
    scol = lax.broadcasted_iota(jnp.int32, (sub * sub, c_len), 1)
    sub_col = lax.broadcasted_iota(jnp.int32, (sub, c_len), 1)
    gn = gn_ref[...]

    def chunk(c, carry):
        r0 = pl.multiple_of(c * c_len, c_len)
        g = la_ref[pl.ds(r0, c_len), :]
        b = jnp.dot(tril, g, precision=lax.Precision.HIGHEST,
                    preferred_element_type=F32)
        q = q_ref[pl.ds(r0, c_len), :].astype(F32) * scale
        k = k_ref[pl.ds(r0, c_len), :].astype(F32)
        v = v_ref[pl.ds(r0, c_len), :]
        b_last = b[c_len - 1:c_len, :]

        st = st_ref[...]
        o = _dot_nt((q * jnp.exp(b)).astype(BF16), st.astype(BF16))
        k_dec = (k * jnp.exp(b_last - b)).astype(BF16)
        st_ref[...] = st * jnp.exp(b_last) + _dot_tn(v, k_dec)

        blocks = []
        for s in range(n_sub):
            lo = s * sub
            b_s = b[lo:lo + sub, :]
            q_s = q[lo:lo + sub, :]
            stacked = []
            for jj in range(sub):
                b_j = b[lo + jj:lo + jj + 1, :]
                stacked.append((q_s * jnp.exp(jnp.minimum(b_s - b_j, 0.0))).astype(BF16))
            stacked = jnp.concatenate(stacked, axis=0)
            full = _dot_nt(stacked, k.astype(BF16))
            full = jnp.where(scol == lo + srow // sub, full, 0.0)
            diag = jnp.sum(full.reshape(sub, sub, c_len), axis=0)
            if s == 0:
                blocks.append(diag)
            else:
                b_ref = b[lo:lo + 1, :]
                q_t = (q_s * jnp.exp(b_s - b_ref)).astype(BF16)
                k_t = (k * jnp.exp(jnp.minimum(b_ref - b, 0.0))).astype(BF16)
                off = _dot_nt(q_t, k_t)
                blocks.append(jnp.where(sub_col < lo, off, diag))
        scores = jnp.concatenate(blocks, axis=0)
        scores = jnp.where(col <= row, scores, 0.0)
        o = o + _dot(scores.astype(BF16), v)

        o = _rms(o, gn)
        r = r_ref[pl.ds(r0, c_len), :].astype(F32)
        out_ref[pl.ds(r0, c_len), :] = (o * (r * jax.nn.sigmoid(r))).astype(out_ref.dtype)
        return carry

    lax.fori_loop(0, n_chunks, chunk, 0)


def _gla(qkvr, log_a, gnorm, *, batch, seq, heads, dk_h, dv_h):
    t = qkvr.shape[0]
    tt = _tile(seq, TILES["gla_tokens"])
    ns = seq // tt
    nq = (heads * dk_h) // dk_h
    v_off = (2 * heads * dk_h) // dv_h
    r_off = v_off + heads

    def rows(b, h, i):
        return b * ns + i

    kern = functools.partial(_gla_kernel, n_chunks=tt // GLA_CHUNK, scale=float(dk_h) ** -0.5)
    return pl.pallas_call(
        kern,
        grid=(batch, heads, ns),
        in_specs=[
            pl.BlockSpec((tt, dk_h), lambda b, h, i: (rows(b, h, i), h)),
            pl.BlockSpec((tt, dk_h), lambda b, h, i: (rows(b, h, i), nq + h)),
            pl.BlockSpec((tt, dv_h), lambda b, h, i: (rows(b, h, i), v_off + h)),
            pl.BlockSpec((tt, dv_h), lambda b, h, i: (rows(b, h, i), r_off + h)),
            pl.BlockSpec((tt, dk_h), lambda b, h, i: (rows(b, h, i), h)),
            pl.BlockSpec((1, dv_h), lambda b, h, i: (0, 0)),
        ],
        out_specs=pl.BlockSpec((tt, dv_h), lambda b, h, i: (rows(b, h, i), h)),
        out_shape=jax.ShapeDtypeStruct((t, heads * dv_h), BF16),
        scratch_shapes=[pltpu.VMEM((dv_h, dk_h), F32)],
        compiler_params=_cparams(("parallel", "parallel", "arbitrary")),
        name="gla_chunk",
    )(qkvr, qkvr, qkvr, qkvr, log_a, gnorm)


def _outproj_kernel(h_ref, a_ref, w_ref, out_ref):
    out_ref[...] = h_ref[...] + _dot(a_ref[...], w_ref[...])


def _outproj(h, a, w):
    t, d = h.shape
    k = a.shape[1]
    tm = _tile(t, TILES["outproj_m"])
    return pl.pallas_call(
        _outproj_kernel,
        grid=(t // tm,),
        in_specs=[
            pl.BlockSpec((tm, d), lambda i: (i, 0)),
            pl.BlockSpec((tm, k), lambda i: (i, 0)),
            pl.BlockSpec((k, d), lambda i: (0, 0)),
        ],
        out_specs=pl.BlockSpec((tm, d), lambda i: (i, 0)),
        out_shape=jax.ShapeDtypeStruct((t, d), F32),
        compiler_params=_cparams(("parallel",)),
        name="outproj",
    )(h, a, w)


def _mlp_kernel(x_ref, g_ref, w1_ref, w2_ref, gf_ref, out_ref, xn_ref, *, final_norm):
    j = pl.program_id(1)

    @pl.when(j == 0)
    def _():
        x = x_ref[...]
        xn_ref[...] = _rms(x, g_ref[...]).astype(BF16)
        out_ref[...] = x

    hid = _dot(xn_ref[...], w1_ref[...])
    hid = jnp.square(jnp.maximum(hid, 0.0)).astype(BF16)
    out_ref[...] += _dot(hid, w2_ref[...])

    if final_norm:
        @pl.when(j == pl.num_programs(1) - 1)
        def _():
            out_ref[...] = _rms(out_ref[...], gf_ref[...])


def _mlp(h, g, w1, w2, gf, *, final_norm):
    t, d = h.shape
    f = w1.shape[1]
    tm = _tile(t, TILES["mlp_m"])
    tf = _tile(f, TILES["mlp_f"])
    return pl.pallas_call(
        functools.partial(_mlp_kernel, final_norm=final_norm),
        grid=(t // tm, f // tf),
        in_specs=[
            pl.BlockSpec((tm, d), lambda i, j: (i, 0)),
            pl.BlockSpec((1, d), lambda i, j: (0, 0)),
            pl.BlockSpec((d, tf), lambda i, j: (0, j)),
            pl.BlockSpec((tf, d), lambda i, j: (j, 0)),
            pl.BlockSpec((1, d), lambda i, j: (0, 0)),
        ],
        out_specs=pl.BlockSpec((tm, d), lambda i, j: (i, 0)),
        out_shape=jax.ShapeDtypeStruct((t, d), F32),
        scratch_shapes=[pltpu.VMEM((tm, d), BF16)],
        compiler_params=_cparams(("parallel", "arbitrary")),
        name="mlp",
    )(h, g, w1, w2, gf)


def _rope_table_kernel(pos_ref, freq_ref, sign_ref, keep_ref, cos_ref, sin_ref):
    ang = pos_ref[...].astype(F32) * freq_ref[...]
    cos_ref[...] = jnp.cos(ang) * keep_ref[...]
    sin_ref[...] = jnp.sin(ang) * sign_ref[...]


def _rope_tables(pos, rope_dim):
    t = pos.shape[0]
    half = rope_dim // 2
    freqs = ROPE_THETA ** (-jnp.arange(0, rope_dim, 2, dtype=F32) / rope_dim)
    pad = jnp.zeros((LANES - rope_dim,), F32)
    freq = jnp.concatenate([freqs, freqs, pad]).reshape(1, LANES)
    sign = jnp.concatenate([-jnp.ones((half,), F32), jnp.ones((half,), F32), pad]).reshape(1, LANES)
    keep = jnp.concatenate([jnp.ones((rope_dim,), F32), pad]).reshape(1, LANES)
    tm = _tile(t, TILES["rope_m"])
    row = pl.BlockSpec((1, LANES), lambda i: (0, 0))
    tab = pl.BlockSpec((tm, LANES), lambda i: (i, 0))
    return pl.pallas_call(
        _rope_table_kernel,
        grid=(t // tm,),
        in_specs=[pl.BlockSpec((tm, 1), lambda i: (i, 0)), row, row, row],
        out_specs=[tab, tab],
        out_shape=[jax.ShapeDtypeStruct((t, LANES), F32)] * 2,
        compiler_params=_cparams(("parallel",)),
        name="rope_tables",
    )(pos, freq, sign, keep)


def _mla_kv_kernel(x_ref, g_ref, wd_ref, gkv_ref, wuk_ref, wuv_ref, cos_ref, sin_ref,
                   k_ref, v_ref, *, heads, lora):
    xn = _rms(x_ref[...], g_ref[...]).astype(BF16)
    ckv = _dot(xn, wd_ref[...])
    c = _rms(ckv[:, :lora], gkv_ref[...]).astype(BF16)
    k_rope = ckv[:, lora:lora + LANES] * cos_ref[...] + ckv[:, lora + LANES:] * sin_ref[...]
    k_rope = k_rope.astype(BF16)
    k_nope = _dot(c, wuk_ref[...]).astype(BF16)
    v_ref[...] = _dot(c, wuv_ref[...]).astype(BF16)
    for h in range(heads):
        k_ref[:, 2 * h * LANES:(2 * h + 1) * LANES] = k_nope[:, h * LANES:(h + 1) * LANES]
        k_ref[:, (2 * h + 1) * LANES:(2 * h + 2) * LANES] = k_rope


def _mla_kv(h, g, wd, gkv, wuk, wuv, cos_t, sin_t, *, heads):
    t, d = h.shape
    lora = gkv.shape[1]
    tm = _tile(t, TILES["kv_m"])
    full = lambda a: pl.BlockSpec(a.shape, lambda i: (0, 0))
    return pl.pallas_call(
        functools.partial(_mla_kv_kernel, heads=heads, lora=lora),
        grid=(t // tm,),
        in_specs=[
            pl.BlockSpec((tm, d), lambda i: (i, 0)),
            full(g), full(wd), full(gkv), full(wuk), full(wuv),
            pl.BlockSpec((tm, LANES), lambda i: (i, 0)),
            pl.BlockSpec((tm, LANES), lambda i: (i, 0)),
        ],
        out_specs=[
            pl.BlockSpec((tm, 2 * heads * LANES), lambda i: (i, 0)),
            pl.BlockSpec((tm, heads * LANES), lambda i: (i, 0)),
        ],
        out_shape=[
            jax.ShapeDtypeStruct((t, 2 * heads * LANES), BF16),
            jax.ShapeDtypeStruct((t, heads * LANES), BF16),
        ],
        compiler_params=_cparams(("parallel",)),
        name="mla_kv",
    )(h, g, wd, gkv, wuk, wuv, cos_t, sin_t)


def _mla_q_kernel(x_ref, g_ref, wd_ref, gq_ref, wn_ref, wr_ref, ws_ref, cos_ref, sin_ref,
                  q_ref, *, heads, scale):
    xn = _rms(x_ref[...], g_ref[...]).astype(BF16)
    cq = _rms(_dot(xn, wd_ref[...]), gq_ref[...]).astype(BF16)
    q_nope = _dot(cq, wn_ref[...])
    q_rope = _dot(cq, wr_ref[...])
    q_swap = _dot(cq, ws_ref[...])
    cos_t = cos_ref[...]
    sin_t = sin_ref[...]
    for h in range(heads):
        sl = slice(h * LANES, (h + 1) * LANES)
        q_ref[:, 2 * h * LANES:(2 * h + 1) * LANES] = (q_nope[:, sl] * scale).astype(BF16)
        roped = q_rope[:, sl] * cos_t + q_swap[:, sl] * sin_t
        q_ref[:, (2 * h + 1) * LANES:(2 * h + 2) * LANES] = (roped * scale).astype(BF16)


def _mla_q(h, g, wd, gq, wn, wr, ws, cos_t, sin_t, *, heads, scale):
    t, d = h.shape
    tm = _tile(t, TILES["q_m"])
    full = lambda a: pl.BlockSpec(a.shape, lambda i: (0, 0))
    return pl.pallas_call(
        functools.partial(_mla_q_kernel, heads=heads, scale=scale),
        grid=(t // tm,),
        in_specs=[
            pl.BlockSpec((tm, d), lambda i: (i, 0)),
            full(g), full(wd), full(gq), full(wn), full(wr), full(ws),
            pl.BlockSpec((tm, LANES), lambda i: (i, 0)),
            pl.BlockSpec((tm, LANES), lambda i: (i, 0)),
        ],
        out_specs=pl.BlockSpec((tm, 2 * heads * LANES), lambda i: (i, 0)),
        out_shape=jax.ShapeDtypeStruct((t, 2 * heads * LANES), BF16),
        compiler_params=_cparams(("parallel",)),
        name="mla_q",
    )(h, g, wd, gq, wn, wr, ws, cos_t, sin_t)


def _attn_kernel(qi_ref, kj_ref, q_ref, k_ref, v_ref, out_ref, m_ref, acc_ref, *, group, tq, tk):
    p = pl.program_id(2)
    qi = qi_ref[p]
    kj = kj_ref[p]
    q0 = qi * tq
    k0 = kj * tk

    @pl.when(kj == 0)
    def _():
        m_ref[...] = jnp.full_like(m_ref, NEG_BIG)
        acc_ref[...] = jnp.zeros_like(acc_ref)

    ones = jnp.ones((tk, LANES), BF16)

    def step(masked):
        if masked:
            row = q0 + lax.broadcasted_iota(jnp.int32, (tq, tk), 0)
            col = k0 + lax.broadcasted_iota(jnp.int32, (tq, tk), 1)
            keep = col <= row
        for g in range(group):
            q = q_ref[:, 2 * g * LANES:(2 * g + 2) * LANES]
            k = k_ref[:, 2 * g * LANES:(2 * g + 2) * LANES]
            s = _dot_nt(q, k)
            if masked:
                s = jnp.where(keep, s, NEG_BIG)
            m_prev = m_ref[g]
            m_new = jnp.maximum(m_prev, jnp.max(s, axis=1, keepdims=True))
            alpha = jnp.exp2(m_prev - m_new)
            pr = jnp.exp2(s - jnp.concatenate([m_new] * (tk // LANES), axis=1))
            v_ext = jnp.concatenate([v_ref[:, g * LANES:(g + 1) * LANES], ones], axis=1)
            acc_ref[g] = jnp.concatenate([alpha, alpha], axis=1) * acc_ref[g] + _dot(pr.astype(BF16), v_ext)
            m_ref[g] = m_new

    crosses = k0 + tk - 1 > q0

    @pl.when(crosses)
    def _():
        step(True)

    @pl.when(jnp.logical_not(crosses))
    def _():
        step(False)

    @pl.when(k0 + tk >= q0 + tq)
    def _():
        for g in range(group):
            acc = acc_ref[g]
            out_ref[:, g * LANES:(g + 1) * LANES] = (acc[:, :LANES] / acc[:, LANES:]).astype(out_ref.dtype)


def _attention(q, k, v, *, batch, seq, heads):
    t = q.shape[0]
    tq = _tile(seq, TILES["attn_q"])
    tk = _tile(tq, TILES["attn_k"])
    group = _tile(heads, TILES["attn_heads"])
    nq, nk = seq // tq, seq // tk
    pairs = [(i, j) for i in range(nq) for j in range(((i + 1) * tq - 1) // tk + 1)]
    qi_tab = jnp.asarray([a for a, _ in pairs], jnp.int32)
    kj_tab = jnp.asarray([b for _, b in pairs], jnp.int32)

    grid_spec = pltpu.PrefetchScalarGridSpec(
        num_scalar_prefetch=2,
        grid=(batch, heads // group, len(pairs)),
        in_specs=[
            pl.BlockSpec((tq, 2 * group * LANES), lambda b, h, p, qi, kj: (b * nq + qi[p], h)),
            pl.BlockSpec((tk, 2 * group * LANES), lambda b, h, p, qi, kj: (b * nk + kj[p], h)),
            pl.BlockSpec((tk, group * LANES), lambda b, h, p, qi, kj: (b * nk + kj[p], h)),
        ],
        out_specs=pl.BlockSpec((tq, group * LANES), lambda b, h, p, qi, kj: (b * nq + qi[p], h)),
        scratch_shapes=[pltpu.VMEM((group, tq, LANES), F32), pltpu.VMEM((group, tq, 2 * LANES), F32)],
    )
    return pl.pallas_call(
        functools.partial(_attn_kernel, group=group, tq=tq, tk=tk),
        grid_spec=grid_spec,
        out_shape=jax.ShapeDtypeStruct((t, heads * LANES), BF16),
        compiler_params=_cparams(("parallel", "parallel", "arbitrary")),
        name="mla_attention",
    )(qi_tab, kj_tab, q, k, v)


def _pad_cols(w, n):
    return jnp.pad(w, ((0, 0), (0, n - w.shape[1])))


def _swap_halves(w):
    half = w.shape[-1] // 2
    return jnp.concatenate([w[..., half:], w[..., :half]], axis=-1)


def _pad_heads(w, heads):
    kdim = w.shape[0]
    w = w.reshape(kdim, heads, -1)
    w = jnp.pad(w, ((0, 0), (0, 0), (0, LANES - w.shape[-1])))
    return w.reshape(kdim, heads * LANES)


def kernel(x, positions, norm_mix, norm_mlp, gla_w_in, gla_w_gate_up, gla_b_gate, gla_norm, gla_w_out,
           kv_norm_in, mla_w_dkv, mla_kv_norm, mla_w_uk, mla_w_uv, mla_w_dq, mla_q_norm, mla_w_uq,
           mla_w_o, mlp_w1, mlp_w2, final_norm):
    batch, seq, d = x.shape
    t = batch * seq
    depth = norm_mix.shape[0]
    n_gla = gla_w_in.shape[0]

    gla_dk = gla_w_gate_up.shape[2]
    gla_dv = gla_w_out.shape[1]
    gla_dv_h = gla_norm.shape[1]
    gla_heads = gla_dv // gla_dv_h
    gla_dk_h = gla_dk // gla_heads
    n_main = 2 * gla_dk + 2 * gla_dv

    kv_lora = mla_kv_norm.shape[0]
    rope_dim = mla_w_dkv.shape[1] - kv_lora
    mla_heads = (mla_w_uq.shape[2] - mla_w_uk.shape[1]) // rope_dim
    nope_dim = mla_w_uk.shape[1] // mla_heads
    assert nope_dim == LANES and mla_w_uv.shape[1] == mla_heads * LANES and rope_dim <= LANES
    assert gla_dk_h % LANES == 0 and gla_dv_h % LANES == 0
    attn_scale = float(nope_dim + rope_dim) ** -0.5 * LOG2_E

    h = x.reshape(t, d)
    pos = positions.reshape(t, 1)
    row = lambda a: a.reshape(1, -1)

    cos_t = sin_t = k_cat = v_all = None
    for layer in range(depth):
        if layer < n_gla:
            a = layer
            w_in = gla_w_in[a]
            w_main = w_in[:, :n_main].astype(BF16)
            w_g = _pad_cols(w_in[:, n_main:], LANES).astype(BF16)
            w_up = jnp.pad(gla_w_gate_up[a], ((0, LANES - gla_w_gate_up.shape[1]), (0, 0))).astype(BF16)
            qkvr, log_a = _gla_inproj(h, row(norm_mix[layer]), w_main, w_g, w_up, row(gla_b_gate[a]))
            o = _gla(qkvr, log_a, row(gla_norm[a]), batch=batch, seq=seq, heads=gla_heads,
                     dk_h=gla_dk_h, dv_h=gla_dv_h)
            h = _outproj(h, o, gla_w_out[a].astype(BF16))
        else:
            b = layer - n_gla
            if layer == n_gla:
                cos_t, sin_t = _rope_tables(pos, rope_dim)
                w_lat = mla_w_dkv[:, :kv_lora]
                w_rope = mla_w_dkv[:, kv_lora:]
                w_dkv = jnp.concatenate(
                    [w_lat, _pad_cols(w_rope, LANES), _pad_cols(_swap_halves(w_rope), LANES)], axis=1)
                k_cat, v_all = _mla_kv(h, row(kv_norm_in), w_dkv.astype(BF16), row(mla_kv_norm),
                                       mla_w_uk.astype(BF16), mla_w_uv.astype(BF16), cos_t, sin_t,
                                       heads=mla_heads)
            w_uq = mla_w_uq[b].reshape(-1, mla_heads, nope_dim + rope_dim)
            q_lora = w_uq.shape[0]
            w_n = w_uq[:, :, :nope_dim].reshape(q_lora, mla_heads * nope_dim)
            w_r = w_uq[:, :, nope_dim:]
            w_rp = _pad_heads(w_r.reshape(q_lora, -1), mla_heads)
            w_sp = _pad_heads(_swap_halves(w_r).reshape(q_lora, -1), mla_heads)
            q_cat = _mla_q(h, row(norm_mix[layer]), mla_w_dq[b].astype(BF16), row(mla_q_norm[b]),
                           w_n.astype(BF16), w_rp.astype(BF16), w_sp.astype(BF16), cos_t, sin_t,
                           heads=mla_heads, scale=attn_scale)
            o = _attention(q_cat, k_cat, v_all, batch=batch, seq=seq, heads=mla_heads)
            h = _outproj(h, o, mla_w_o[b].astype(BF16))
        h = _mlp(h, row(norm_mlp[layer]), mlp_w1[layer].astype(BF16), mlp_w2[layer].astype(BF16),
                 row(final_norm), final_norm=(layer == depth - 1))
    return h.reshape(batch, seq, d)
```

```python
import functools

import jax
import jax.numpy as jnp
from jax import lax
from jax.experimental import pallas as pl
from jax.experimental.pallas import tpu as pltpu

BF16 = jnp.bfloat16
F32 = jnp.float32

RMS_EPS = 1e-6
ROPE_THETA = 10000.0
GLA_TAU = 16.0
GLA_CHUNK = 64
GLA_SUB = 16
LANES = 128
NEG_BIG = -1e30
LOG2_E = 1.4426950408889634

VMEM_LIMIT_BYTES = 56 * 1024 * 1024

TILES = dict(
    inproj_m=1024, inproj_n=1024,
    gla_tokens=512,
    outproj_m=512,
    mlp_m=1024, mlp_f=512,
    kv_m=512,
    q_m=512,
    rope_m=2048,
    attn_q=1024, attn_k=1024, attn_heads=4,
)


def _cparams(sem):
    return pltpu.CompilerParams(dimension_semantics=sem, vmem_limit_bytes=VMEM_LIMIT_BYTES)


def _tile(n, t):
    t = min(n, t)
    assert n % t == 0, (n, t)
    return t


def _rms(x, g):
    return x * lax.rsqrt(jnp.mean(x * x, axis=-1, keepdims=True) + RMS_EPS) * g


def _dot(a, b):
    return jnp.dot(a, b, preferred_element_type=F32)


def _dot_nt(a, b):
    return lax.dot_general(a, b, (((1,), (1,)), ((), ())), preferred_element_type=F32)


def _dot_tn(a, b):
    return lax.dot_general(a, b, (((0,), (0,)), ((), ())), preferred_element_type=F32)


def _gla_inproj_kernel(x_ref, g_ref, w_ref, wg_ref, wup_ref, bg_ref, out_ref, la_ref, xn_ref):
    @pl.when(pl.program_id(1) == 0)
    def _():
        xn = _rms(x_ref[...], g_ref[...]).astype(BF16)
        xn_ref[...] = xn
        g_low = _dot(xn, wg_ref[...])
        z = _dot(g_low.astype(BF16), wup_ref[...]) + bg_ref[...]
        log_sig = jnp.minimum(z, 0.0) - jnp.log(1.0 + jnp.exp(-jnp.abs(z)))
        la_ref[...] = log_sig * (LOG2_E / GLA_TAU)

    out_ref[...] = _dot(xn_ref[...], w_ref[...]).astype(out_ref.dtype)


def _gla_inproj(h, g, w, wg, wup, bg):
    t, d = h.shape
    n = w.shape[1]
    dk = wup.shape[1]
    tm = _tile(t, TILES["inproj_m"])
    tn = _tile(n, TILES["inproj_n"])
    return pl.pallas_call(
        _gla_inproj_kernel,
        grid=(t // tm, n // tn),
        in_specs=[
            pl.BlockSpec((tm, d), lambda i, j: (i, 0)),
            pl.BlockSpec((1, d), lambda i, j: (0, 0)),
            pl.BlockSpec((d, tn), lambda i, j: (0, j)),
            pl.BlockSpec(wg.shape, lambda i, j: (0, 0)),
            pl.BlockSpec(wup.shape, lambda i, j: (0, 0)),
            pl.BlockSpec((1, dk), lambda i, j: (0, 0)),
        ],
        out_specs=[
            pl.BlockSpec((tm, tn), lambda i, j: (i, j)),
            pl.BlockSpec((tm, dk), lambda i, j: (i, 0)),
        ],
        out_shape=[
            jax.ShapeDtypeStruct((t, n), BF16),
            jax.ShapeDtypeStruct((t, dk), F32),
        ],
        scratch_shapes=[pltpu.VMEM((tm, d), BF16)],
        compiler_params=_cparams(("parallel", "arbitrary")),
        name="gla_inproj",
    )(h, g, w, wg, wup, bg)


def _gla_kernel(x_ref, la_ref, gn_ref, out_ref, st_ref, *, n_chunks, heads, dk_h, dv_h, scale):
    c_len, sub = GLA_CHUNK, GLA_SUB
    n_sub = c_len // sub
    k_off = heads * dk_h
    v_off = 2 * heads * dk_h
    r_off = v_off + heads * dv_h

    @pl.when(pl.program_id(1) == 0)
    def _():
        st_ref[...] = jnp.zeros_like(st_ref)

    row = lax.broadcasted_iota(jnp.int32, (c_len, c_len), 0)
    col = lax.broadcasted_iota(jnp.int32, (c_len, c_len), 1)
    tril = (col <= row).astype(F32)
    causal = col <= row
    srow = lax.broadcasted_iota(jnp.int32, (sub * sub, c_len), 0)
    scol = lax.broadcasted_iota(jnp.int32, (sub * sub, c_len), 1)
    diag_keep = [scol == s * sub + srow // sub for s in range(n_sub)]
    sub_col = lax.broadcasted_iota(jnp.int32, (sub, c_len), 1)
    below = [sub_col < s * sub for s in range(n_sub)]
    gn = gn_ref[...]

    def head(h, rows, b):
        q = x_ref[rows, h * dk_h:(h + 1) * dk_h].astype(F32) * scale
        k = x_ref[rows, k_off + h * dk_h:k_off + (h + 1) * dk_h]
        k_f = k.astype(F32)
        v = x_ref[rows, v_off + h * dv_h:v_off + (h + 1) * dv_h]
        b_last = b[c_len - 1:c_len, :]

        st = st_ref[h]
        o = _dot_nt((q * jnp.exp2(b)).astype(BF16), st.astype(BF16))
        k_dec = (k_f * jnp.exp2(b_last - b)).astype(BF16)
        st_ref[h] = st * jnp.exp2(b_last) + _dot_tn(v, k_dec)

        blocks = []
        for s in range(n_sub):
            lo = s * sub
            b_s = b[lo:lo + sub, :]
            q_s = q[lo:lo + sub, :]
            stacked = [(q_s * jnp.exp2(b_s - b[lo + jj:lo + jj + 1, :])).astype(BF16) for jj in range(sub)]
            full = _dot_nt(jnp.concatenate(stacked, axis=0), k)
            full = jnp.where(diag_keep[s], full, 0.0)
            diag = jnp.sum(full.reshape(sub, sub, c_len), axis=0)
            if s == 0:
                blocks.append(diag)
            else:
                b_ref = b[lo:lo + 1, :]
                q_t = (q_s * jnp.exp2(b_s - b_ref)).astype(BF16)
                k_t = (k_f * jnp.exp2(b_ref - b)).astype(BF16)
                blocks.append(jnp.where(below[s], _dot_nt(q_t, k_t), diag))
        scores = jnp.where(causal, jnp.concatenate(blocks, axis=0), 0.0)
        o = _rms(o + _dot(scores.astype(BF16), v), gn)
        r = x_ref[rows, r_off + h * dv_h:r_off + (h + 1) * dv_h].astype(F32)
        out_ref[rows, h * dv_h:(h + 1) * dv_h] = (o * (r * jax.nn.sigmoid(r))).astype(out_ref.dtype)

    def chunk(c, carry):
        rows = pl.ds(pl.multiple_of(c * c_len, c_len), c_len)
        b_all = jnp.dot(tril, la_ref[rows, :], precision=lax.Precision.HIGHEST,
                        preferred_element_type=F32)
        for h in range(heads):
            head(h, rows, b_all[:, h * dk_h:(h + 1) * dk_h])
        return carry

    lax.fori_loop(0, n_chunks, chunk, 0)


def _gla(qkvr, log_a, gnorm, *, batch, seq, heads, dk_h, dv_h):
    t, n = qkvr.shape
    tt = _tile(seq, TILES["gla_tokens"])
    ns = seq // tt
    kern = functools.partial(_gla_kernel, n_chunks=tt // GLA_CHUNK, heads=heads, dk_h=dk_h, dv_h=dv_h,
                             scale=float(dk_h) ** -0.5)
    return pl.pallas_call(
        kern,
        grid=(batch, ns),
        in_specs=[
            pl.BlockSpec((tt, n), lambda b, i: (b * ns + i, 0)),
            pl.BlockSpec((tt, heads * dk_h), lambda b, i: (b * ns + i, 0)),
            pl.BlockSpec((1, dv_h), lambda b, i: (0, 0)),
        ],
        out_specs=pl.BlockSpec((tt, heads * dv_h), lambda b, i: (b * ns + i, 0)),
        out_shape=jax.ShapeDtypeStruct((t, heads * dv_h), BF16),
        scratch_shapes=[pltpu.VMEM((heads, dv_h, dk_h), F32)],
        compiler_params=_cparams(("parallel", "arbitrary")),
        name="gla_chunk",
    )(qkvr, log_a, gnorm)


def _outproj_kernel(h_ref, a_ref, w_ref, out_ref):
    out_ref[...] = h_ref[...] + _dot(a_ref[...], w_ref[...])


def _outproj(h, a, w):
    t, d = h.shape
    k = a.shape[1]
    tm = _tile(t, TILES["outproj_m"])
    return pl.pallas_call(
        _outproj_kernel,
        grid=(t // tm,),
        in_specs=[
            pl.BlockSpec((tm, d), lambda i: (i, 0)),
            pl.BlockSpec((tm, k), lambda i: (i, 0)),
            pl.BlockSpec((k, d), lambda i: (0, 0)),
        ],
        out_specs=pl.BlockSpec((tm, d), lambda i: (i, 0)),
        out_shape=jax.ShapeDtypeStruct((t, d), F32),
        compiler_params=_cparams(("parallel",)),
        name="outproj",
    )(h, a, w)


def _mlp_kernel(x_ref, g_ref, w1_ref, w2_ref, gf_ref, out_ref, xn_ref, *, final_norm):
    j = pl.program_id(1)

    @pl.when(j == 0)
    def _():
        x = x_ref[...]
        xn_ref[...] = _rms(x, g_ref[...]).astype(BF16)
        out_ref[...] = x

    hid = _dot(xn_ref[...], w1_ref[...])
    hid = jnp.square(jnp.maximum(hid, 0.0)).astype(BF16)
    out_ref[...] += _dot(hid, w2_ref[...])

    if final_norm:
        @pl.when(j == pl.num_programs(1) - 1)
        def _():
            out_ref[...] = _rms(out_ref[...], gf_ref[...])


def _mlp(h, g, w1, w2, gf, *, final_norm):
    t, d = h.shape
    f = w1.shape[1]
    tm = _tile(t, TILES["mlp_m"])
    tf = _tile(f, TILES["mlp_f"])
    return pl.pallas_call(
        functools.partial(_mlp_kernel, final_norm=final_norm),
        grid=(t // tm, f // tf),
        in_specs=[
            pl.BlockSpec((tm, d), lambda i, j: (i, 0)),
            pl.BlockSpec((1, d), lambda i, j: (0, 0)),
            pl.BlockSpec((d, tf), lambda i, j: (0, j)),
            pl.BlockSpec((tf, d), lambda i, j: (j, 0)),
            pl.BlockSpec((1, d), lambda i, j: (0, 0)),
        ],
        out_specs=pl.BlockSpec((tm, d), lambda i, j: (i, 0)),
        out_shape=jax.ShapeDtypeStruct((t, d), F32),
        scratch_shapes=[pltpu.VMEM((tm, d), BF16)],
        compiler_params=_cparams(("parallel", "arbitrary")),
        name="mlp",
    )(h, g, w1, w2, gf)


def _rope_table_kernel(pos_ref, freq_ref, sign_ref, keep_ref, cos_ref, sin_ref):
    ang = pos_ref[...].astype(F32) * freq_ref[...]
    cos_ref[...] = jnp.cos(ang) * keep_ref[...]
    sin_ref[...] = jnp.sin(ang) * sign_ref[...]


def _rope_tables(pos, rope_dim):
    t = pos.shape[0]
    half = rope_dim // 2
    freqs = ROPE_THETA ** (-jnp.arange(0, rope_dim, 2, dtype=F32) / rope_dim)
    pad = jnp.zeros((LANES - rope_dim,), F32)
    freq = jnp.concatenate([freqs, freqs, pad]).reshape(1, LANES)
    sign = jnp.concatenate([-jnp.ones((half,), F32), jnp.ones((half,), F32), pad]).reshape(1, LANES)
    keep = jnp.concatenate([jnp.ones((rope_dim,), F32), pad]).reshape(1, LANES)
    tm = _tile(t, TILES["rope_m"])
    row = pl.BlockSpec((1, LANES), lambda i: (0, 0))
    tab = pl.BlockSpec((tm, LANES), lambda i: (i, 0))
    return pl.pallas_call(
        _rope_table_kernel,
        grid=(t // tm,),
        in_specs=[pl.BlockSpec((tm, 1), lambda i: (i, 0)), row, row, row],
        out_specs=[tab, tab],
        out_shape=[jax.ShapeDtypeStruct((t, LANES), F32)] * 2,
        compiler_params=_cparams(("parallel",)),
        name="rope_tables",
    )(pos, freq, sign, keep)


def _mla_kv_kernel(x_ref, g_ref, wd_ref, gkv_ref, wuk_ref, wuv_ref, cos_ref, sin_ref,
                   k_ref, v_ref, *, heads, lora):
    xn = _rms(x_ref[...], g_ref[...]).astype(BF16)
    ckv = _dot(xn, wd_ref[...])
    c = _rms(ckv[:, :lora], gkv_ref[...]).astype(BF16)
    k_rope = ckv[:, lora:lora + LANES] * cos_ref[...] + ckv[:, lora + LANES:] * sin_ref[...]
    k_rope = k_rope.astype(BF16)
    k_nope = _dot(c, wuk_ref[...]).astype(BF16)
    v_ref[...] = _dot(c, wuv_ref[...]).astype(BF16)
    for h in range(heads):
        k_ref[:, 2 * h * LANES:(2 * h + 1) * LANES] = k_nope[:, h * LANES:(h + 1) * LANES]
        k_ref[:, (2 * h + 1) * LANES:(2 * h + 2) * LANES] = k_rope


def _mla_kv(h, g, wd, gkv, wuk, wuv, cos_t, sin_t, *, heads):
    t, d = h.shape
    lora = gkv.shape[1]
    tm = _tile(t, TILES["kv_m"])
    full = lambda a: pl.BlockSpec(a.shape, lambda i: (0, 0))
    return pl.pallas_call(
        functools.partial(_mla_kv_kernel, heads=heads, lora=lora),
        grid=(t // tm,),
        in_specs=[
            pl.BlockSpec((tm, d), lambda i: (i, 0)),
            full(g), full(wd), full(gkv), full(wuk), full(wuv),
            pl.BlockSpec((tm, LANES), lambda i: (i, 0)),
            pl.BlockSpec((tm, LANES), lambda i: (i, 0)),
        ],
        out_specs=[
            pl.BlockSpec((tm, 2 * heads * LANES), lambda i: (i, 0)),
            pl.BlockSpec((tm, heads * LANES), lambda i: (i, 0)),
        ],
        out_shape=[
            jax.ShapeDtypeStruct((t, 2 * heads * LANES), BF16),
            jax.ShapeDtypeStruct((t, heads * LANES), BF16),
        ],
        compiler_params=_cparams(("parallel",)),
        name="mla_kv",
    )(h, g, wd, gkv, wuk, wuv, cos_t, sin_t)


def _mla_q_kernel(x_ref, g_ref, wd_ref, gq_ref, wn_ref, wr_ref, ws_ref, cos_ref, sin_ref,
                  q_ref, *, heads, scale):
    xn = _rms(x_ref[...], g_ref[...]).astype(BF16)
    cq = _rms(_dot(xn, wd_ref[...]), gq_ref[...]).astype(BF16)
    q_nope = _dot(cq, wn_ref[...])
    q_rope = _dot(cq, wr_ref[...])
    q_swap = _dot(cq, ws_ref[...])
    cos_t = cos_ref[...]
    sin_t = sin_ref[...]
    for h in range(heads):
        sl = slice(h * LANES, (h + 1) * LANES)
        q_ref[:, 2 * h * LANES:(2 * h + 1) * LANES] = (q_nope[:, sl] * scale).astype(BF16)
        roped = q_rope[:, sl] * cos_t + q_swap[:, sl] * sin_t
        q_ref[:, (2 * h + 1) * LANES:(2 * h + 2) * LANES] = (roped * scale).astype(BF16)


def _mla_q(h, g, wd, gq, wn, wr, ws, cos_t, sin_t, *, heads, scale):
    t, d = h.shape
    tm = _tile(t, TILES["q_m"])
    full = lambda a: pl.BlockSpec(a.shape, lambda i: (0, 0))
    return pl.pallas_call(
        functools.partial(_mla_q_kernel, heads=heads, scale=scale),
        grid=(t // tm,),
        in_specs=[
            pl.BlockSpec((tm, d), lambda i: (i, 0)),
            full(g), full(wd), full(gq), full(wn), full(wr), full(ws),
            pl.BlockSpec((tm, LANES), lambda i: (i, 0)),
            pl.BlockSpec((tm, LANES), lambda i: (i, 0)),
        ],
        out_specs=pl.BlockSpec((tm, 2 * heads * LANES), lambda i: (i, 0)),
        out_shape=jax.ShapeDtypeStruct((t, 2 * heads * LANES), BF16),
        compiler_params=_cparams(("parallel",)),
        name="mla_q",
    )(h, g, wd, gq, wn, wr, ws, cos_t, sin_t)


def _attn_kernel(qi_ref, kj_ref, q_ref, k_ref, v_ref, out_ref, m_ref, acc_ref, *, group, tq, tk):
    p = pl.program_id(2)
    qi = qi_ref[p]
    kj = kj_ref[p]
    q0 = qi * tq
    k0 = kj * tk

    @pl.when(kj == 0)
    def _():
        m_ref[...] = jnp.full_like(m_ref, NEG_BIG)
        acc_ref[...] = jnp.zeros_like(acc_ref)

    ones = jnp.ones((tk, LANES), BF16)

    def step(masked):
        if masked:
            row = q0 + lax.broadcasted_iota(jnp.int32, (tq, tk), 0)
            col = k0 + lax.broadcasted_iota(jnp.int32, (tq, tk), 1)
            keep = col <= row
        for g in range(group):
            q = q_ref[:, 2 * g * LANES:(2 * g + 2) * LANES]
            k = k_ref[:, 2 * g * LANES:(2 * g + 2) * LANES]
            s = _dot_nt(q, k)
            if masked:
                s = jnp.where(keep, s, NEG_BIG)
            m_prev = m_ref[g]
            m_new = jnp.maximum(m_prev, jnp.max(s, axis=1, keepdims=True))
            alpha = jnp.exp2(m_prev - m_new)
            pr = jnp.exp2(s - jnp.concatenate([m_new] * (tk // LANES), axis=1))
            v_ext = jnp.concatenate([v_ref[:, g * LANES:(g + 1) * LANES], ones], axis=1)
            acc_ref[g] = jnp.concatenate([alpha, alpha], axis=1) * acc_ref[g] + _dot(pr.astype(BF16), v_ext)
            m_ref[g] = m_new

    crosses = k0 + tk - 1 > q0

    @pl.when(crosses)
    def _():
        step(True)

    @pl.when(jnp.logical_not(crosses))
    def _():
        step(False)

    @pl.when(k0 + tk >= q0 + tq)
    def _():
        for g in range(group):
            acc = acc_ref[g]
            out_ref[:, g * LANES:(g + 1) * LANES] = (acc[:, :LANES] / acc[:, LANES:]).astype(out_ref.dtype)


def _attention(q, k, v, *, batch, seq, heads):
    t = q.shape[0]
    tq = _tile(seq, TILES["attn_q"])
    tk = _tile(tq, TILES["attn_k"])
    group = _tile(heads, TILES["attn_heads"])
    nq, nk = seq // tq, seq // tk
    pairs = [(i, j) for i in range(nq) for j in range(((i + 1) * tq - 1) // tk + 1)]
    qi_tab = jnp.asarray([a for a, _ in pairs], jnp.int32)
    kj_tab = jnp.asarray([b for _, b in pairs], jnp.int32)

    grid_spec = pltpu.PrefetchScalarGridSpec(
        num_scalar_prefetch=2,
        grid=(batch, heads // group, len(pairs)),
        in_specs=[
            pl.BlockSpec((tq, 2 * group * LANES), lambda b, h, p, qi, kj: (b * nq + qi[p], h)),
            pl.BlockSpec((tk, 2 * group * LANES), lambda b, h, p, qi, kj: (b * nk + kj[p], h)),
            pl.BlockSpec((tk, group * LANES), lambda b, h, p, qi, kj: (b * nk + kj[p], h)),
        ],
        out_specs=pl.BlockSpec((tq, group * LANES), lambda b, h, p, qi, kj: (b * nq + qi[p], h)),
        scratch_shapes=[pltpu.VMEM((group, tq, LANES), F32), pltpu.VMEM((group, tq, 2 * LANES), F32)],
    )
    return pl.pallas_call(
        functools.partial(_attn_kernel, group=group, tq=tq, tk=tk),
        grid_spec=grid_spec,
        out_shape=jax.ShapeDtypeStruct((t, heads * LANES), BF16),
        compiler_params=_cparams(("parallel", "parallel", "arbitrary")),
        name="mla_attention",
    )(qi_tab, kj_tab, q, k, v)


def _pad_cols(w, n):
    return jnp.pad(w, ((0, 0), (0, n - w.shape[1])))


def _swap_halves(w):
    half = w.shape[-1] // 2
    return jnp.concatenate([w[..., half:], w[..., :half]], axis=-1)


def _pad_heads(w, heads):
    kdim = w.shape[0]
    w = w.reshape(kdim, heads, -1)
    w = jnp.pad(w, ((0, 0), (0, 0), (0, LANES - w.shape[-1])))
    return w.reshape(kdim, heads * LANES)


def kernel(x, positions, norm_mix, norm_mlp, gla_w_in, gla_w_gate_up, gla_b_gate, gla_norm, gla_w_out,
           kv_norm_in, mla_w_dkv, mla_kv_norm, mla_w_uk, mla_w_uv, mla_w_dq, mla_q_norm, mla_w_uq,
           mla_w_o, mlp_w1, mlp_w2, final_norm):
    batch, seq, d = x.shape
    t = batch * seq
    depth = norm_mix.shape[0]
    n_gla = gla_w_in.shape[0]

    gla_dk = gla_w_gate_up.shape[2]
    gla_dv = gla_w_out.shape[1]
    gla_dv_h = gla_norm.shape[1]
    gla_heads = gla_dv // gla_dv_h
    gla_dk_h = gla_dk // gla_heads
    n_main = 2 * gla_dk + 2 * gla_dv

    kv_lora = mla_kv_norm.shape[0]
    rope_dim = mla_w_dkv.shape[1] - kv_lora
    mla_heads = (mla_w_uq.shape[2] - mla_w_uk.shape[1]) // rope_dim
    nope_dim = mla_w_uk.shape[1] // mla_heads
    assert nope_dim == LANES and mla_w_uv.shape[1] == mla_heads * LANES and rope_dim <= LANES
    assert gla_dk_h % LANES == 0 and gla_dv_h % LANES == 0
    attn_scale = float(nope_dim + rope_dim) ** -0.5 * LOG2_E

    h = x.reshape(t, d)
    pos = positions.reshape(t, 1)
    row = lambda a: a.reshape(1, -1)

    cos_t = sin_t = k_cat = v_all = None
    for layer in range(depth):
        if layer < n_gla:
            a = layer
            w_in = gla_w_in[a]
            w_main = w_in[:, :n_main].astype(BF16)
            w_g = _pad_cols(w_in[:, n_main:], LANES).astype(BF16)
            w_up = jnp.pad(gla_w_gate_up[a], ((0, LANES - gla_w_gate_up.shape[1]), (0, 0))).astype(BF16)
            qkvr, log_a = _gla_inproj(h, row(norm_mix[layer]), w_main, w_g, w_up, row(gla_b_gate[a]))
            o = _gla(qkvr, log_a, row(gla_norm[a]), batch=batch, seq=seq, heads=gla_heads,
                     dk_h=gla_dk_h, dv_h=gla_dv_h)
            h = _outproj(h, o, gla_w_out[a].astype(BF16))
        else:
            b = layer - n_gla
            if layer == n_gla:
                cos_t, sin_t = _rope_tables(pos, rope_dim)
                w_lat = mla_w_dkv[:, :kv_lora]
                w_rope = mla_w_dkv[:, kv_lora:]
                w_dkv = jnp.concatenate(
                    [w_lat, _pad_cols(w_rope, LANES), _pad_cols(_swap_halves(w_rope), LANES)], axis=1)
                k_cat, v_all = _mla_kv(h, row(kv_norm_in), w_dkv.astype(BF16), row(mla_kv_norm),
                                       mla_w_uk.astype(BF16), mla_w_uv.astype(BF16), cos_t, sin_t,
                                       heads=mla_heads)
            w_uq = mla_w_uq[b].reshape(-1, mla_heads, nope_dim + rope_dim)
            q_lora = w_uq.shape[0]
            w_n = w_uq[:, :, :nope_dim].reshape(q_lora, mla_heads * nope_dim)
            w_r = w_uq[:, :, nope_dim:]
            w_rp = _pad_heads(w_r.reshape(q_lora, -1), mla_heads)
            w_sp = _pad_heads(_swap_halves(w_r).reshape(q_lora, -1), mla_heads)
            q_cat = _mla_q(h, row(norm_mix[layer]), mla_w_dq[b].astype(BF16), row(mla_q_norm[b]),
                           w_n.astype(BF16), w_rp.astype(BF16), w_sp.astype(BF16), cos_t, sin_t,
                           heads=mla_heads, scale=attn_scale)
            o = _attention(q_cat, k_cat, v_all, batch=batch, seq=seq, heads=mla_heads)
            h = _outproj(h, o, mla_w_o[b].astype(BF16))
        h = _mlp(h, row(norm_mlp[layer]), mlp_w1[layer].astype(BF16), mlp_w2[layer].astype(BF16),
                 row(final_norm), final_norm=(layer == depth - 1))
    return h.reshape(batch, seq, d)
```

```python
import functools

import jax
import jax.numpy as jnp
from jax import lax
from jax.experimental import pallas as pl
from jax.experimental.pallas import tpu as pltpu

BF16 = jnp.bfloat16
F32 = jnp.float32

RMS_EPS = 1e-6
ROPE_THETA = 10000.0
GLA_TAU = 16.0
GLA_CHUNK = 64
GLA_SUB = 16
LANES = 128
NEG_BIG = -1e30
LOG2_E = 1.4426950408889634
ATTN_ROW_SPLIT = 2

VMEM_LIMIT_BYTES = 56 * 1024 * 1024

TILES = dict(
    inproj_m=1024, inproj_n=1024,
    gla_tokens=512,
    outproj_m=512,
    mlp_m=1024, mlp_f=512,
    kv_m=512,
    q_m=512,
    rope_m=2048,
    attn_q=1024, attn_k=1024, attn_heads=4,
)


def _cparams(sem):
    return pltpu.CompilerParams(dimension_semantics=sem, vmem_limit_bytes=VMEM_LIMIT_BYTES)


def _tile(n, t):
    t = min(n, t)
    assert n % t == 0, (n, t)
    return t


def _rms(x, g):
    return x * lax.rsqrt(jnp.mean(x * x, axis=-1, keepdims=True) + RMS_EPS) * g


def _dot(a, b):
    return jnp.dot(a, b, preferred_element_type=F32)


def _dot_nt(a, b):
    return lax.dot_general(a, b, (((1,), (1,)), ((), ())), preferred_element_type=F32)


def _dot_tn(a, b):
    return lax.dot_general(a, b, (((0,), (0,)), ((), ())), preferred_element_type=F32)


def _gla_inproj_kernel(x_ref, g_ref, w_ref, wg_ref, wup_ref, bg_ref, out_ref, la_ref, xn_ref):
    @pl.when(pl.program_id(1) == 0)
    def _():
        xn = _rms(x_ref[...], g_ref[...]).astype(BF16)
        xn_ref[...] = xn
        g_low = _dot(xn, wg_ref[...])
        z = _dot(g_low.astype(BF16), wup_ref[...]) + bg_ref[...]
        log_sig = jnp.minimum(z, 0.0) - jnp.log(1.0 + jnp.exp(-jnp.abs(z)))
        la_ref[...] = log_sig * (LOG2_E / GLA_TAU)

    out_ref[...] = _dot(xn_ref[...], w_ref[...]).astype(out_ref.dtype)


def _gla_inproj(h, g, w, wg, wup, bg):
    t, d = h.shape
    n = w.shape[1]
    dk = wup.shape[1]
    tm = _tile(t, TILES["inproj_m"])
    tn = _tile(n, TILES["inproj_n"])
    return pl.pallas_call(
        _gla_inproj_kernel,
        grid=(t // tm, n // tn),
        in_specs=[
            pl.BlockSpec((tm, d), lambda i, j: (i, 0)),
            pl.BlockSpec((1, d), lambda i, j: (0, 0)),
            pl.BlockSpec((d, tn), lambda i, j: (0, j)),
            pl.BlockSpec(wg.shape, lambda i, j: (0, 0)),
            pl.BlockSpec(wup.shape, lambda i, j: (0, 0)),
            pl.BlockSpec((1, dk), lambda i, j: (0, 0)),
        ],
        out_specs=[
            pl.BlockSpec((tm, tn), lambda i, j: (i, j)),
            pl.BlockSpec((tm, dk), lambda i, j: (i, 0)),
        ],
        out_shape=[
            jax.ShapeDtypeStruct((t, n), BF16),
            jax.ShapeDtypeStruct((t, dk), F32),
        ],
        scratch_shapes=[pltpu.VMEM((tm, d), BF16)],
        compiler_params=_cparams(("parallel", "arbitrary")),
        name="gla_inproj",
    )(h, g, w, wg, wup, bg)


def _gla_kernel(x_ref, la_ref, gn_ref, out_ref, st_ref, *, n_chunks, heads, dk_h, dv_h, scale):
    c_len, sub = GLA_CHUNK, GLA_SUB
    n_sub = c_len // sub
    k_off = heads * dk_h
    v_off = 2 * heads * dk_h
    r_off = v_off + heads * dv_h

    @pl.when(pl.program_id(1) == 0)
    def _():
        st_ref[...] = jnp.zeros_like(st_ref)

    row = lax.broadcasted_iota(jnp.int32, (c_len, c_len), 0)
    col = lax.broadcasted_iota(jnp.int32, (c_len, c_len), 1)
    tril = (col <= row).astype(F32)
    causal = col <= row
    srow = lax.broadcasted_iota(jnp.int32, (sub * sub, c_len), 0)
    scol = lax.broadcasted_iota(jnp.int32, (sub * sub, c_len), 1)
    diag_keep = [scol == s * sub + srow // sub for s in range(n_sub)]
    sub_col = lax.broadcasted_iota(jnp.int32, (sub, c_len), 1)
    below = [sub_col < s * sub for s in range(n_sub)]
    gn = gn_ref[...]

    def head(h, rows, b):
        q = x_ref[rows, h * dk_h:(h + 1) * dk_h].astype(F32) * scale
        k = x_ref[rows, k_off + h * dk_h:k_off + (h + 1) * dk_h]
        k_f = k.astype(F32)
        v = x_ref[rows, v_off + h * dv_h:v_off + (h + 1) * dv_h]
        b_last = b[c_len - 1:c_len, :]

        st = st_ref[h]
        o = _dot_nt((q * jnp.exp2(b)).astype(BF16), st.astype(BF16))
        k_dec = (k_f * jnp.exp2(b_last - b)).astype(BF16)
        st_ref[h] = st * jnp.exp2(b_last) + _dot_tn(v, k_dec)

        blocks = []
        for s in range(n_sub):
            lo = s * sub
            b_s = b[lo:lo + sub, :]
            q_s = q[lo:lo + sub, :]
            stacked = [(q_s * jnp.exp2(b_s - b[lo + jj:lo + jj + 1, :])).astype(BF16) for jj in range(sub)]
            full = _dot_nt(jnp.concatenate(stacked, axis=0), k)
            full = jnp.where(diag_keep[s], full, 0.0)
            diag = jnp.sum(full.reshape(sub, sub, c_len), axis=0)
            if s == 0:
                blocks.append(diag)
            else:
                b_ref = b[lo:lo + 1, :]
                q_t = (q_s * jnp.exp2(b_s - b_ref)).astype(BF16)
                k_t = (k_f * jnp.exp2(b_ref - b)).astype(BF16)
                blocks.append(jnp.where(below[s], _dot_nt(q_t, k_t), diag))
        scores = jnp.where(causal, jnp.concatenate(blocks, axis=0), 0.0)
        o = _rms(o + _dot(scores.astype(BF16), v), gn)
        r = x_ref[rows, r_off + h * dv_h:r_off + (h + 1) * dv_h].astype(F32)
        out_ref[rows, h * dv_h:(h + 1) * dv_h] = (o * (r * jax.nn.sigmoid(r))).astype(out_ref.dtype)

    def chunk(c, carry):
        rows = pl.ds(pl.multiple_of(c * c_len, c_len), c_len)
        b_all = jnp.dot(tril, la_ref[rows, :], precision=lax.Precision.HIGHEST,
                        preferred_element_type=F32)
        for h in range(heads):
            head(h, rows, b_all[:, h * dk_h:(h + 1) * dk_h])
        return carry

    lax.fori_loop(0, n_chunks, chunk, 0)


def _gla(qkvr, log_a, gnorm, *, batch, seq, heads, dk_h, dv_h):
    t, n = qkvr.shape
    tt = _tile(seq, TILES["gla_tokens"])
    ns = seq // tt
    kern = functools.partial(_gla_kernel, n_chunks=tt // GLA_CHUNK, heads=heads, dk_h=dk_h, dv_h=dv_h,
                             scale=float(dk_h) ** -0.5)
    return pl.pallas_call(
        kern,
        grid=(batch, ns),
        in_specs=[
            pl.BlockSpec((tt, n), lambda b, i: (b * ns + i, 0)),
            pl.BlockSpec((tt, heads * dk_h), lambda b, i: (b * ns + i, 0)),
            pl.BlockSpec((1, dv_h), lambda b, i: (0, 0)),
        ],
        out_specs=pl.BlockSpec((tt, heads * dv_h), lambda b, i: (b * ns + i, 0)),
        out_shape=jax.ShapeDtypeStruct((t, heads * dv_h), BF16),
        scratch_shapes=[pltpu.VMEM((heads, dv_h, dk_h), F32)],
        compiler_params=_cparams(("parallel", "arbitrary")),
        name="gla_chunk",
    )(qkvr, log_a, gnorm)


def _outproj_kernel(h_ref, a_ref, w_ref, out_ref):
    out_ref[...] = h_ref[...] + _dot(a_ref[...], w_ref[...])


def _outproj(h, a, w):
    t, d = h.shape
    k = a.shape[1]
    tm = _tile(t, TILES["outproj_m"])
    return pl.pallas_call(
        _outproj_kernel,
        grid=(t // tm,),
        in_specs=[
            pl.BlockSpec((tm, d), lambda i: (i, 0)),
            pl.BlockSpec((tm, k), lambda i: (i, 0)),
            pl.BlockSpec((k, d), lambda i: (0, 0)),
        ],
        out_specs=pl.BlockSpec((tm, d), lambda i: (i, 0)),
        out_shape=jax.ShapeDtypeStruct((t, d), F32),
        compiler_params=_cparams(("parallel",)),
        name="outproj",
    )(h, a, w)


def _mlp_kernel(x_ref, g_ref, w1_ref, w2_ref, gf_ref, out_ref, xn_ref, *, final_norm):
    j = pl.program_id(1)

    @pl.when(j == 0)
    def _():
        x = x_ref[...]
        xn_ref[...] = _rms(x, g_ref[...]).astype(BF16)
        out_ref[...] = x

    hid = _dot(xn_ref[...], w1_ref[...])
    hid = jnp.square(jnp.maximum(hid, 0.0)).astype(BF16)
    out_ref[...] += _dot(hid, w2_ref[...])

    if final_norm:
        @pl.when(j == pl.num_programs(1) - 1)
        def _():
            out_ref[...] = _rms(out_ref[...], gf_ref[...])


def _mlp(h, g, w1, w2, gf, *, final_norm):
    t, d = h.shape
    f = w1.shape[1]
    tm = _tile(t, TILES["mlp_m"])
    tf = _tile(f, TILES["mlp_f"])
    return pl.pallas_call(
        functools.partial(_mlp_kernel, final_norm=final_norm),
        grid=(t // tm, f // tf),
        in_specs=[
            pl.BlockSpec((tm, d), lambda i, j: (i, 0)),
            pl.BlockSpec((1, d), lambda i, j: (0, 0)),
            pl.BlockSpec((d, tf), lambda i, j: (0, j)),
            pl.BlockSpec((tf, d), lambda i, j: (j, 0)),
            pl.BlockSpec((1, d), lambda i, j: (0, 0)),
        ],
        out_specs=pl.BlockSpec((tm, d), lambda i, j: (i, 0)),
        out_shape=jax.ShapeDtypeStruct((t, d), F32),
        scratch_shapes=[pltpu.VMEM((tm, d), BF16)],
        compiler_params=_cparams(("parallel", "arbitrary")),
        name="mlp",
    )(h, g, w1, w2, gf)


def _rope_table_kernel(pos_ref, freq_ref, sign_ref, keep_ref, cos_ref, sin_ref):
    ang = pos_ref[...].astype(F32) * freq_ref[...]
    cos_ref[...] = jnp.cos(ang) * keep_ref[...]
    sin_ref[...] = jnp.sin(ang) * sign_ref[...]


def _rope_tables(pos, rope_dim):
    t = pos.shape[0]
    half = rope_dim // 2
    freqs = ROPE_THETA ** (-jnp.arange(0, rope_dim, 2, dtype=F32) / rope_dim)
    pad = jnp.zeros((LANES - rope_dim,), F32)
    freq = jnp.concatenate([freqs, freqs, pad]).reshape(1, LANES)
    sign = jnp.concatenate([-jnp.ones((half,), F32), jnp.ones((half,), F32), pad]).reshape(1, LANES)
    keep = jnp.concatenate([jnp.ones((rope_dim,), F32), pad]).reshape(1, LANES)
    tm = _tile(t, TILES["rope_m"])
    row = pl.BlockSpec((1, LANES), lambda i: (0, 0))
    tab = pl.BlockSpec((tm, LANES), lambda i: (i, 0))
    return pl.pallas_call(
        _rope_table_kernel,
        grid=(t // tm,),
        in_specs=[pl.BlockSpec((tm, 1), lambda i: (i, 0)), row, row, row],
        out_specs=[tab, tab],
        out_shape=[jax.ShapeDtypeStruct((t, LANES), F32)] * 2,
        compiler_params=_cparams(("parallel",)),
        name="rope_tables",
    )(pos, freq, sign, keep)


def _mla_kv_kernel(x_ref, g_ref, wd_ref, gkv_ref, wuk_ref, wuv_ref, cos_ref, sin_ref,
                   k_ref, v_ref, *, heads, lora):
    xn = _rms(x_ref[...], g_ref[...]).astype(BF16)
    ckv = _dot(xn, wd_ref[...])
    c = _rms(ckv[:, :lora], gkv_ref[...]).astype(BF16)
    k_rope = ckv[:, lora:lora + LANES] * cos_ref[...] + ckv[:, lora + LANES:] * sin_ref[...]
    k_rope = k_rope.astype(BF16)
    k_nope = _dot(c, wuk_ref[...]).astype(BF16)
    v_ref[...] = _dot(c, wuv_ref[...]).astype(BF16)
    for h in range(heads):
        k_ref[:, 2 * h * LANES:(2 * h + 1) * LANES] = k_nope[:, h * LANES:(h + 1) * LANES]
        k_ref[:, (2 * h + 1) * LANES:(2 * h + 2) * LANES] = k_rope


def _mla_kv(h, g, wd, gkv, wuk, wuv, cos_t, sin_t, *, heads):
    t, d = h.shape
    lora = gkv.shape[1]
    tm = _tile(t, TILES["kv_m"])
    full = lambda a: pl.BlockSpec(a.shape, lambda i: (0, 0))
    return pl.pallas_call(
        functools.partial(_mla_kv_kernel, heads=heads, lora=lora),
        grid=(t // tm,),
        in_specs=[
            pl.BlockSpec((tm, d), lambda i: (i, 0)),
            full(g), full(wd), full(gkv), full(wuk), full(wuv),
            pl.BlockSpec((tm, LANES), lambda i: (i, 0)),
            pl.BlockSpec((tm, LANES), lambda i: (i, 0)),
        ],
        out_specs=[
            pl.BlockSpec((tm, 2 * heads * LANES), lambda i: (i, 0)),
            pl.BlockSpec((tm, heads * LANES), lambda i: (i, 0)),
        ],
        out_shape=[
            jax.ShapeDtypeStruct((t, 2 * heads * LANES), BF16),
            jax.ShapeDtypeStruct((t, heads * LANES), BF16),
        ],
        compiler_params=_cparams(("parallel",)),
        name="mla_kv",
    )(h, g, wd, gkv, wuk, wuv, cos_t, sin_t)


def _mla_q_kernel(x_ref, g_ref, wd_ref, gq_ref, wn_ref, wr_ref, ws_ref, cos_ref, sin_ref,
                  q_ref, *, heads, scale):
    xn = _rms(x_ref[...], g_ref[...]).astype(BF16)
    cq = _rms(_dot(xn, wd_ref[...]), gq_ref[...]).astype(BF16)
    q_nope = _dot(cq, wn_ref[...])
    q_rope = _dot(cq, wr_ref[...])
    q_swap = _dot(cq, ws_ref[...])
    cos_t = cos_ref[...]
    sin_t = sin_ref[...]
    for h in range(heads):
        sl = slice(h * LANES, (h + 1) * LANES)
        q_ref[:, 2 * h * LANES:(2 * h + 1) * LANES] = (q_nope[:, sl] * scale).astype(BF16)
        roped = q_rope[:, sl] * cos_t + q_swap[:, sl] * sin_t
        q_ref[:, (2 * h + 1) * LANES:(2 * h + 2) * LANES] = (roped * scale).astype(BF16)


def _mla_q(h, g, wd, gq, wn, wr, ws, cos_t, sin_t, *, heads, scale):
    t, d = h.shape
    tm = _tile(t, TILES["q_m"])
    full = lambda a: pl.BlockSpec(a.shape, lambda i: (0, 0))
    return pl.pallas_call(
        functools.partial(_mla_q_kernel, heads=heads, scale=scale),
        grid=(t // tm,),
        in_specs=[
            pl.BlockSpec((tm, d), lambda i: (i, 0)),
            full(g), full(wd), full(gq), full(wn), full(wr), full(ws),
            pl.BlockSpec((tm, LANES), lambda i: (i, 0)),
            pl.BlockSpec((tm, LANES), lambda i: (i, 0)),
        ],
        out_specs=pl.BlockSpec((tm, 2 * heads * LANES), lambda i: (i, 0)),
        out_shape=jax.ShapeDtypeStruct((t, 2 * heads * LANES), BF16),
        compiler_params=_cparams(("parallel",)),
        name="mla_q",
    )(h, g, wd, gq, wn, wr, ws, cos_t, sin_t)


def _attn_kernel(qi_ref, kj_ref, q_ref, k_ref, v_ref, out_ref, m_ref, acc_ref, *, group, tq, tk):
    p = pl.program_id(2)
    qi = qi_ref[p]
    kj = kj_ref[p]
    q0 = qi * tq
    k0 = kj * tk

    @pl.when(kj == 0)
    def _():
        m_ref[...] = jnp.full_like(m_ref, NEG_BIG)
        acc_ref[...] = jnp.zeros_like(acc_ref)

    ones = jnp.ones((tk, LANES), BF16)

    def step(masked):
        if masked:
            row = q0 + lax.broadcasted_iota(jnp.int32, (tq, tk), 0)
            col = k0 + lax.broadcasted_iota(jnp.int32, (tq, tk), 1)
            keep = col <= row

        def scores(g):
            q = q_ref[:, 2 * g * LANES:(2 * g + 2) * LANES]
            k = k_ref[:, 2 * g * LANES:(2 * g + 2) * LANES]
            s = _dot_nt(q, k)
            if masked:
                s = jnp.where(keep, s, NEG_BIG)
            return s

        def update(g, s):
            v_ext = jnp.concatenate([v_ref[:, g * LANES:(g + 1) * LANES], ones], axis=1)
            rb = tq // ATTN_ROW_SPLIT
            for i in range(ATTN_ROW_SPLIT):
                rows = slice(i * rb, (i + 1) * rb)
                s_i = s[rows, :]
                m_prev = m_ref[g, rows, :]
                m_new = jnp.maximum(m_prev, jnp.max(s_i, axis=1, keepdims=True))
                alpha = jnp.exp2(m_prev - m_new)
                pr = jnp.exp2(s_i - jnp.concatenate([m_new] * (tk // LANES), axis=1))
                acc_ref[g, rows, :] = (jnp.concatenate([alpha, alpha], axis=1) * acc_ref[g, rows, :]
                                       + _dot(pr.astype(BF16), v_ext))
                m_ref[g, rows, :] = m_new

        s_prev = scores(0)
        for g in range(1, group):
            s_next = scores(g)
            update(g - 1, s_prev)
            s_prev = s_next
        update(group - 1, s_prev)

    crosses = k0 + tk - 1 > q0

    @pl.when(crosses)
    def _():
        step(True)

    @pl.when(jnp.logical_not(crosses))
    def _():
        step(False)

    @pl.when(k0 + tk >= q0 + tq)
    def _():
        for g in range(group):
            acc = acc_ref[g]
            out_ref[:, g * LANES:(g + 1) * LANES] = (acc[:, :LANES] / acc[:, LANES:]).astype(out_ref.dtype)


def _attention(q, k, v, *, batch, seq, heads):
    t = q.shape[0]
    tq = _tile(seq, TILES["attn_q"])
    tk = _tile(tq, TILES["attn_k"])
    group = _tile(heads, TILES["attn_heads"])
    nq, nk = seq // tq, seq // tk
    pairs = [(i, j) for i in range(nq) for j in range(((i + 1) * tq - 1) // tk + 1)]
    qi_tab = jnp.asarray([a for a, _ in pairs], jnp.int32)
    kj_tab = jnp.asarray([b for _, b in pairs], jnp.int32)

    grid_spec = pltpu.PrefetchScalarGridSpec(
        num_scalar_prefetch=2,
        grid=(batch, heads // group, len(pairs)),
        in_specs=[
            pl.BlockSpec((tq, 2 * group * LANES), lambda b, h, p, qi, kj: (b * nq + qi[p], h)),
            pl.BlockSpec((tk, 2 * group * LANES), lambda b, h, p, qi, kj: (b * nk + kj[p], h)),
            pl.BlockSpec((tk, group * LANES), lambda b, h, p, qi, kj: (b * nk + kj[p], h)),
        ],
        out_specs=pl.BlockSpec((tq, group * LANES), lambda b, h, p, qi, kj: (b * nq + qi[p], h)),
        scratch_shapes=[pltpu.VMEM((group, tq, LANES), F32), pltpu.VMEM((group, tq, 2 * LANES), F32)],
    )
    return pl.pallas_call(
        functools.partial(_attn_kernel, group=group, tq=tq, tk=tk),
        grid_spec=grid_spec,
        out_shape=jax.ShapeDtypeStruct((t, heads * LANES), BF16),
        compiler_params=_cparams(("parallel", "parallel", "arbitrary")),
        name="mla_attention",
    )(qi_tab, kj_tab, q, k, v)


def _pad_cols(w, n):
    return jnp.pad(w, ((0, 0), (0, n - w.shape[1])))


def _swap_halves(w):
    half = w.shape[-1] // 2
    return jnp.concatenate([w[..., half:], w[..., :half]], axis=-1)


def _pad_heads(w, heads):
    kdim = w.shape[0]
    w = w.reshape(kdim, heads, -1)
    w = jnp.pad(w, ((0, 0), (0, 0), (0, LANES - w.shape[-1])))
    return w.reshape(kdim, heads * LANES)


def kernel(x, positions, norm_mix, norm_mlp, gla_w_in, gla_w_gate_up, gla_b_gate, gla_norm, gla_w_out,
           kv_norm_in, mla_w_dkv, mla_kv_norm, mla_w_uk, mla_w_uv, mla_w_dq, mla_q_norm, mla_w_uq,
           mla_w_o, mlp_w1, mlp_w2, final_norm):
    batch, seq, d = x.shape
    t = batch * seq
    depth = norm_mix.shape[0]
    n_gla = gla_w_in.shape[0]

    gla_dk = gla_w_gate_up.shape[2]
    gla_dv = gla_w_out.shape[1]
    gla_dv_h = gla_norm.shape[1]
    gla_heads = gla_dv // gla_dv_h
    gla_dk_h = gla_dk // gla_heads
    n_main = 2 * gla_dk + 2 * gla_dv

    kv_lora = mla_kv_norm.shape[0]
    rope_dim = mla_w_dkv.shape[1] - kv_lora
    mla_heads = (mla_w_uq.shape[2] - mla_w_uk.shape[1]) // rope_dim
    nope_dim = mla_w_uk.shape[1] // mla_heads
    assert nope_dim == LANES and mla_w_uv.shape[1] == mla_heads * LANES and rope_dim <= LANES
    assert gla_dk_h % LANES == 0 and gla_dv_h % LANES == 0
    attn_scale = float(nope_dim + rope_dim) ** -0.5 * LOG2_E

    h = x.reshape(t, d)
    pos = positions.reshape(t, 1)
    row = lambda a: a.reshape(1, -1)

    cos_t = sin_t = k_cat = v_all = None
    for layer in range(depth):
        if layer < n_gla:
            a = layer
            w_in = gla_w_in[a]
            w_main = w_in[:, :n_main].astype(BF16)
            w_g = _pad_cols(w_in[:, n_main:], LANES).astype(BF16)
            w_up = jnp.pad(gla_w_gate_up[a], ((0, LANES - gla_w_gate_up.shape[1]), (0, 0))).astype(BF16)
            qkvr, log_a = _gla_inproj(h, row(norm_mix[layer]), w_main, w_g, w_up, row(gla_b_gate[a]))
            o = _gla(qkvr, log_a, row(gla_norm[a]), batch=batch, seq=seq, heads=gla_heads,
                     dk_h=gla_dk_h, dv_h=gla_dv_h)
            h = _outproj(h, o, gla_w_out[a].astype(BF16))
        else:
            b = layer - n_gla
            if layer == n_gla:
                cos_t, sin_t = _rope_tables(pos, rope_dim)
                w_lat = mla_w_dkv[:, :kv_lora]
                w_rope = mla_w_dkv[:, kv_lora:]
                w_dkv = jnp.concatenate(
                    [w_lat, _pad_cols(w_rope, LANES), _pad_cols(_swap_halves(w_rope), LANES)], axis=1)
                k_cat, v_all = _mla_kv(h, row(kv_norm_in), w_dkv.astype(BF16), row(mla_kv_norm),
                                       mla_w_uk.astype(BF16), mla_w_uv.astype(BF16), cos_t, sin_t,
                                       heads=mla_heads)
            w_uq = mla_w_uq[b].reshape(-1, mla_heads, nope_dim + rope_dim)
            q_lora = w_uq.shape[0]
            w_n = w_uq[:, :, :nope_dim].reshape(q_lora, mla_heads * nope_dim)
            w_r = w_uq[:, :, nope_dim:]
            w_rp = _pad_heads(w_r.reshape(q_lora, -1), mla_heads)
            w_sp = _pad_heads(_swap_halves(w_r).reshape(q_lora, -1), mla_heads)
            q_cat = _mla_q(h, row(norm_mix[layer]), mla_w_dq[b].astype(BF16), row(mla_q_norm[b]),
                           w_n.astype(BF16), w_rp.astype(BF16), w_sp.astype(BF16), cos_t, sin_t,
                           heads=mla_heads, scale=attn_scale)
            o = _attention(q_cat, k_cat, v_all, batch=batch, seq=seq, heads=mla_heads)
            h = _outproj(h, o, mla_w_o[b].astype(BF16))
        h = _mlp(h, row(norm_mlp[layer]), mlp_w1[layer].astype(BF16), mlp_w2[layer].astype(BF16),
                 row(final_norm), final_norm=(layer == depth - 1))
    return h.reshape(batch, seq, d)
```

```python
import functools

import jax
import jax.numpy as jnp
from jax import lax
from jax.experimental import pallas as pl
from jax.experimental.pallas import tpu as pltpu

BF16 = jnp.bfloat16
F32 = jnp.float32

RMS_EPS = 1e-6
ROPE_THETA = 10000.0
GLA_TAU = 16.0
GLA_CHUNK = 64
GLA_SUB = 8
LANES = 128
NEG_BIG = -1e30
LOG2_E = 1.4426950408889634
ATTN_ROW_SPLIT = 2

VMEM_LIMIT_BYTES = 56 * 1024 * 1024

TILES = dict(
    inproj_m=1024, inproj_n=1024,
    gla_tokens=512,
    outproj_m=512,
    mlp_m=1024, mlp_f=512,
    kv_m=512,
    q_m=512,
    rope_m=2048,
    attn_q=1024, attn_k=1024, attn_heads=4,
)


def _cparams(sem):
    return pltpu.CompilerParams(dimension_semantics=sem, vmem_limit_bytes=VMEM_LIMIT_BYTES)


def _tile(n, t):
    t = min(n, t)
    assert n % t == 0, (n, t)
    return t


def _rms(x, g):
    return x * lax.rsqrt(jnp.mean(x * x, axis=-1, keepdims=True) + RMS_EPS) * g


def _dot(a, b):
    return jnp.dot(a, b, preferred_element_type=F32)


def _dot_nt(a, b):
    return lax.dot_general(a, b, (((1,), (1,)), ((), ())), preferred_element_type=F32)


def _dot_tn(a, b):
    return lax.dot_general(a, b, (((0,), (0,)), ((), ())), preferred_element_type=F32)


def _gla_inproj_kernel(x_ref, g_ref, w_ref, wg_ref, wup_ref, bg_ref, out_ref, la_ref, xn_ref):
    @pl.when(pl.program_id(1) == 0)
    def _():
        xn = _rms(x_ref[...], g_ref[...]).astype(BF16)
        xn_ref[...] = xn
        g_low = _dot(xn, wg_ref[...])
        z = _dot(g_low.astype(BF16), wup_ref[...]) + bg_ref[...]
        log_sig = jnp.minimum(z, 0.0) - jnp.log(1.0 + jnp.exp(-jnp.abs(z)))
        la_ref[...] = log_sig * (LOG2_E / GLA_TAU)

    out_ref[...] = _dot(xn_ref[...], w_ref[...]).astype(out_ref.dtype)


def _gla_inproj(h, g, w, wg, wup, bg):
    t, d = h.shape
    n = w.shape[1]
    dk = wup.shape[1]
    tm = _tile(t, TILES["inproj_m"])
    tn = _tile(n, TILES["inproj_n"])
    return pl.pallas_call(
        _gla_inproj_kernel,
        grid=(t // tm, n // tn),
        in_specs=[
            pl.BlockSpec((tm, d), lambda i, j: (i, 0)),
            pl.BlockSpec((1, d), lambda i, j: (0, 0)),
            pl.BlockSpec((d, tn), lambda i, j: (0, j)),
            pl.BlockSpec(wg.shape, lambda i, j: (0, 0)),
            pl.BlockSpec(wup.shape, lambda i, j: (0, 0)),
            pl.BlockSpec((1, dk), lambda i, j: (0, 0)),
        ],
        out_specs=[
            pl.BlockSpec((tm, tn), lambda i, j: (i, j)),
            pl.BlockSpec((tm, dk), lambda i, j: (i, 0)),
        ],
        out_shape=[
            jax.ShapeDtypeStruct((t, n), BF16),
            jax.ShapeDtypeStruct((t, dk), F32),
        ],
        scratch_shapes=[pltpu.VMEM((tm, d), BF16)],
        compiler_params=_cparams(("parallel", "arbitrary")),
        name="gla_inproj",
    )(h, g, w, wg, wup, bg)


def _gla_kernel(x_ref, la_ref, gn_ref, out_ref, st_ref, *, n_chunks, heads, dk_h, dv_h, scale):
    c_len, sub = GLA_CHUNK, GLA_SUB
    k_off = heads * dk_h
    v_off = 2 * heads * dk_h
    r_off = v_off + heads * dv_h

    @pl.when(pl.program_id(1) == 0)
    def _():
        st_ref[...] = jnp.zeros_like(st_ref)

    row = lax.broadcasted_iota(jnp.int32, (c_len, c_len), 0)
    col = lax.broadcasted_iota(jnp.int32, (c_len, c_len), 1)
    tril = (col <= row).astype(F32)
    causal = col <= row
    blk = 2 * sub
    n_blk = c_len // blk
    far = col < (row // blk) * blk
    near = jnp.logical_and(jnp.logical_and(row % blk >= sub, col // blk == row // blk), col % blk < sub)
    srow = lax.broadcasted_iota(jnp.int32, (sub * sub, c_len), 0)
    scol = lax.broadcasted_iota(jnp.int32, (sub * sub, c_len), 1)
    diag_base = scol - srow // sub
    gn = gn_ref[...]

    def load(h, rows):
        q = x_ref[rows, h * dk_h:(h + 1) * dk_h].astype(F32) * scale
        k = x_ref[rows, k_off + h * dk_h:k_off + (h + 1) * dk_h]
        v = x_ref[rows, v_off + h * dv_h:v_off + (h + 1) * dv_h]
        return q, k, k.astype(F32), v

    def inter(h, b, q, k_f, v):
        b_last = b[c_len - 1:c_len, :]
        st = st_ref[h]
        o = _dot_nt((q * jnp.exp2(b)).astype(BF16), st.astype(BF16))
        k_dec = (k_f * jnp.exp2(b_last - b)).astype(BF16)
        st_ref[h] = st * jnp.exp2(b_last) + _dot_tn(v, k_dec)
        return o

    def diag_scores(b, q, k):
        diag = []
        for s in range(c_len // sub):
            lo = s * sub
            b_s = b[lo:lo + sub, :]
            q_s = q[lo:lo + sub, :]
            stacked = jnp.concatenate(
                [q_s * jnp.exp2(b_s - b[lo + jj:lo + jj + 1, :]) for jj in range(sub)], axis=0)
            full = _dot_nt(stacked.astype(BF16), k)
            full = jnp.where(diag_base == lo, full, 0.0)
            diag.append(jnp.sum(full.reshape(sub, sub, c_len), axis=0))
        return jnp.concatenate(diag, axis=0)

    def near_scores(b, q, k_f):
        b_mid = jnp.concatenate(
            [jnp.broadcast_to(b[t * blk + sub:t * blk + sub + 1, :], (blk, dk_h)) for t in range(n_blk)], axis=0)
        return _dot_nt((q * jnp.exp2(b - b_mid)).astype(BF16), (k_f * jnp.exp2(b_mid - b)).astype(BF16))

    def far_scores(b, q, k_f):
        far_s = [jnp.zeros((blk, c_len), F32)]
        for t in range(1, n_blk):
            lo = t * blk
            b_ref = b[lo:lo + 1, :]
            q_t = (q[lo:lo + blk, :] * jnp.exp2(b[lo:lo + blk, :] - b_ref)).astype(BF16)
            k_t = jnp.concatenate([k_f[:lo, :] * jnp.exp2(b_ref - b[:lo, :]),
                                   jnp.zeros((c_len - lo, dk_h), F32)], axis=0).astype(BF16)
            far_s.append(_dot_nt(q_t, k_t))
        return jnp.concatenate(far_s, axis=0)

    def emit(h, rows, o, diag, near_s, far_s, v):
        scores = jnp.where(far, far_s, jnp.where(near, near_s, diag))
        scores = jnp.where(causal, scores, 0.0)
        o = _rms(o + _dot(scores.astype(BF16), v), gn)
        r = x_ref[rows, r_off + h * dv_h:r_off + (h + 1) * dv_h].astype(F32)
        out_ref[rows, h * dv_h:(h + 1) * dv_h] = (o * (r * jax.nn.sigmoid(r))).astype(out_ref.dtype)

    def cumsum(c):
        rows = pl.ds(pl.multiple_of(c * c_len, c_len), c_len)
        return jnp.dot(tril, la_ref[rows, :], precision=lax.Precision.HIGHEST,
                       preferred_element_type=F32)

    def chunk(c, b_all):
        b_next = cumsum(jnp.minimum(c + 1, n_chunks - 1))
        rows = pl.ds(pl.multiple_of(c * c_len, c_len), c_len)
        hs = range(heads)
        bs = [b_all[:, h * dk_h:(h + 1) * dk_h] for h in hs]
        qkv = [load(h, rows) for h in hs]
        o = [inter(h, bs[h], qkv[h][0], qkv[h][2], qkv[h][3]) for h in hs]
        prev = None
        for h in hs:
            dg = diag_scores(bs[h], qkv[h][0], qkv[h][1])
            nr = near_scores(bs[h], qkv[h][0], qkv[h][2])
            fr = far_scores(bs[h], qkv[h][0], qkv[h][2])
            if prev is not None:
                emit(*prev)
            prev = (h, rows, o[h], dg, nr, fr, qkv[h][3])
        emit(*prev)
        return b_next

    lax.fori_loop(0, n_chunks, chunk, cumsum(0))


def _gla(qkvr, log_a, gnorm, *, batch, seq, heads, dk_h, dv_h):
    t, n = qkvr.shape
    tt = _tile(seq, TILES["gla_tokens"])
    ns = seq // tt
    kern = functools.partial(_gla_kernel, n_chunks=tt // GLA_CHUNK, heads=heads, dk_h=dk_h, dv_h=dv_h,
                             scale=float(dk_h) ** -0.5)
    return pl.pallas_call(
        kern,
        grid=(batch, ns),
        in_specs=[
            pl.BlockSpec((tt, n), lambda b, i: (b * ns + i, 0)),
            pl.BlockSpec((tt, heads * dk_h), lambda b, i: (b * ns + i, 0)),
            pl.BlockSpec((1, dv_h), lambda b, i: (0, 0)),
        ],
        out_specs=pl.BlockSpec((tt, heads * dv_h), lambda b, i: (b * ns + i, 0)),
        out_shape=jax.ShapeDtypeStruct((t, heads * dv_h), BF16),
        scratch_shapes=[pltpu.VMEM((heads, dv_h, dk_h), F32)],
        compiler_params=_cparams(("parallel", "arbitrary")),
        name="gla_chunk",
    )(qkvr, log_a, gnorm)


def _outproj_kernel(h_ref, a_ref, w_ref, out_ref):
    out_ref[...] = h_ref[...] + _dot(a_ref[...], w_ref[...])


def _outproj(h, a, w):
    t, d = h.shape
    k = a.shape[1]
    tm = _tile(t, TILES["outproj_m"])
    return pl.pallas_call(
        _outproj_kernel,
        grid=(t // tm,),
        in_specs=[
            pl.BlockSpec((tm, d), lambda i: (i, 0)),
            pl.BlockSpec((tm, k), lambda i: (i, 0)),
            pl.BlockSpec((k, d), lambda i: (0, 0)),
        ],
        out_specs=pl.BlockSpec((tm, d), lambda i: (i, 0)),
        out_shape=jax.ShapeDtypeStruct((t, d), F32),
        compiler_params=_cparams(("parallel",)),
        name="outproj",
    )(h, a, w)


def _mlp_kernel(x_ref, g_ref, w1_ref, w2_ref, gf_ref, out_ref, xn_ref, *, final_norm):
    j = pl.program_id(1)

    @pl.when(j == 0)
    def _():
        x = x_ref[...]
        xn_ref[...] = _rms(x, g_ref[...]).astype(BF16)
        out_ref[...] = x

    hid = _dot(xn_ref[...], w1_ref[...])
    hid = jnp.square(jnp.maximum(hid, 0.0)).astype(BF16)
    out_ref[...] += _dot(hid, w2_ref[...])

    if final_norm:
        @pl.when(j == pl.num_programs(1) - 1)
        def _():
            out_ref[...] = _rms(out_ref[...], gf_ref[...])


def _mlp(h, g, w1, w2, gf, *, final_norm):
    t, d = h.shape
    f = w1.shape[1]
    tm = _tile(t, TILES["mlp_m"])
    tf = _tile(f, TILES["mlp_f"])
    return pl.pallas_call(
        functools.partial(_mlp_kernel, final_norm=final_norm),
        grid=(t // tm, f // tf),
        in_specs=[
            pl.BlockSpec((tm, d), lambda i, j: (i, 0)),
            pl.BlockSpec((1, d), lambda i, j: (0, 0)),
            pl.BlockSpec((d, tf), lambda i, j: (0, j)),
            pl.BlockSpec((tf, d), lambda i, j: (j, 0)),
            pl.BlockSpec((1, d), lambda i, j: (0, 0)),
        ],
        out_specs=pl.BlockSpec((tm, d), lambda i, j: (i, 0)),
        out_shape=jax.ShapeDtypeStruct((t, d), F32),
        scratch_shapes=[pltpu.VMEM((tm, d), BF16)],
        compiler_params=_cparams(("parallel", "arbitrary")),
        name="mlp",
    )(h, g, w1, w2, gf)


def _rope_table_kernel(pos_ref, freq_ref, sign_ref, keep_ref, cos_ref, sin_ref):
    ang = pos_ref[...].astype(F32) * freq_ref[...]
    cos_ref[...] = jnp.cos(ang) * keep_ref[...]
    sin_ref[...] = jnp.sin(ang) * sign_ref[...]


def _rope_tables(pos, rope_dim):
    t = pos.shape[0]
    half = rope_dim // 2
    freqs = ROPE_THETA ** (-jnp.arange(0, rope_dim, 2, dtype=F32) / rope_dim)
    pad = jnp.zeros((LANES - rope_dim,), F32)
    freq = jnp.concatenate([freqs, freqs, pad]).reshape(1, LANES)
    sign = jnp.concatenate([-jnp.ones((half,), F32), jnp.ones((half,), F32), pad]).reshape(1, LANES)
    keep = jnp.concatenate([jnp.ones((rope_dim,), F32), pad]).reshape(1, LANES)
    tm = _tile(t, TILES["rope_m"])
    row = pl.BlockSpec((1, LANES), lambda i: (0, 0))
    tab = pl.BlockSpec((tm, LANES), lambda i: (i, 0))
    return pl.pallas_call(
        _rope_table_kernel,
        grid=(t // tm,),
        in_specs=[pl.BlockSpec((tm, 1), lambda i: (i, 0)), row, row, row],
        out_specs=[tab, tab],
        out_shape=[jax.ShapeDtypeStruct((t, LANES), F32)] * 2,
        compiler_params=_cparams(("parallel",)),
        name="rope_tables",
    )(pos, freq, sign, keep)


def _mla_kv_kernel(x_ref, g_ref, wd_ref, gkv_ref, wuk_ref, wuv_ref, cos_ref, sin_ref,
                   k_ref, v_ref, *, heads, lora):
    xn = _rms(x_ref[...], g_ref[...]).astype(BF16)
    ckv = _dot(xn, wd_ref[...])
    c = _rms(ckv[:, :lora], gkv_ref[...]).astype(BF16)
    k_rope = ckv[:, lora:lora + LANES] * cos_ref[...] + ckv[:, lora + LANES:] * sin_ref[...]
    k_rope = k_rope.astype(BF16)
    k_nope = _dot(c, wuk_ref[...]).astype(BF16)
    v_ref[...] = _dot(c, wuv_ref[...]).astype(BF16)
    for h in range(heads):
        k_ref[:, 2 * h * LANES:(2 * h + 1) * LANES] = k_nope[:, h * LANES:(h + 1) * LANES]
        k_ref[:, (2 * h + 1) * LANES:(2 * h + 2) * LANES] = k_rope


def _mla_kv(h, g, wd, gkv, wuk, wuv, cos_t, sin_t, *, heads):
    t, d = h.shape
    lora = gkv.shape[1]
    tm = _tile(t, TILES["kv_m"])
    full = lambda a: pl.BlockSpec(a.shape, lambda i: (0, 0))
    return pl.pallas_call(
        functools.partial(_mla_kv_kernel, heads=heads, lora=lora),
        grid=(t // tm,),
        in_specs=[
            pl.BlockSpec((tm, d), lambda i: (i, 0)),
            full(g), full(wd), full(gkv), full(wuk), full(wuv),
            pl.BlockSpec((tm, LANES), lambda i: (i, 0)),
            pl.BlockSpec((tm, LANES), lambda i: (i, 0)),
        ],
        out_specs=[
            pl.BlockSpec((tm, 2 * heads * LANES), lambda i: (i, 0)),
            pl.BlockSpec((tm, heads * LANES), lambda i: (i, 0)),
        ],
        out_shape=[
            jax.ShapeDtypeStruct((t, 2 * heads * LANES), BF16),
            jax.ShapeDtypeStruct((t, heads * LANES), BF16),
        ],
        compiler_params=_cparams(("parallel",)),
        name="mla_kv",
    )(h, g, wd, gkv, wuk, wuv, cos_t, sin_t)


def _mla_q_kernel(x_ref, g_ref, wd_ref, gq_ref, wn_ref, wr_ref, ws_ref, cos_ref, sin_ref,
                  q_ref, *, heads, scale):
    xn = _rms(x_ref[...], g_ref[...]).astype(BF16)
    cq = _rms(_dot(xn, wd_ref[...]), gq_ref[...]).astype(BF16)
    q_nope = _dot(cq, wn_ref[...])
    q_rope = _dot(cq, wr_ref[...])
    q_swap = _dot(cq, ws_ref[...])
    cos_t = cos_ref[...]
    sin_t = sin_ref[...]
    for h in range(heads):
        sl = slice(h * LANES, (h + 1) * LANES)
        q_ref[:, 2 * h * LANES:(2 * h + 1) * LANES] = (q_nope[:, sl] * scale).astype(BF16)
        roped = q_rope[:, sl] * cos_t + q_swap[:, sl] * sin_t
        q_ref[:, (2 * h + 1) * LANES:(2 * h + 2) * LANES] = (roped * scale).astype(BF16)


def _mla_q(h, g, wd, gq, wn, wr, ws, cos_t, sin_t, *, heads, scale):
    t, d = h.shape
    tm = _tile(t, TILES["q_m"])
    full = lambda a: pl.BlockSpec(a.shape, lambda i: (0, 0))
    return pl.pallas_call(
        functools.partial(_mla_q_kernel, heads=heads, scale=scale),
        grid=(t // tm,),
        in_specs=[
            pl.BlockSpec((tm, d), lambda i: (i, 0)),
            full(g), full(wd), full(gq), full(wn), full(wr), full(ws),
            pl.BlockSpec((tm, LANES), lambda i: (i, 0)),
            pl.BlockSpec((tm, LANES), lambda i: (i, 0)),
        ],
        out_specs=pl.BlockSpec((tm, 2 * heads * LANES), lambda i: (i, 0)),
        out_shape=jax.ShapeDtypeStruct((t, 2 * heads * LANES), BF16),
        compiler_params=_cparams(("parallel",)),
        name="mla_q",
    )(h, g, wd, gq, wn, wr, ws, cos_t, sin_t)


def _attn_kernel(qi_ref, kj_ref, q_ref, k_ref, v_ref, out_ref, m_ref, acc_ref, *, group, tq, tk):
    p = pl.program_id(2)
    qi = qi_ref[p]
    kj = kj_ref[p]
    q0 = qi * tq
    k0 = kj * tk

    @pl.when(kj == 0)
    def _():
        m_ref[...] = jnp.full_like(m_ref, NEG_BIG)
        acc_ref[...] = jnp.zeros_like(acc_ref)

    ones = jnp.ones((tk, LANES), BF16)

    def step(masked):
        if masked:
            row = q0 + lax.broadcasted_iota(jnp.int32, (tq, tk), 0)
            col = k0 + lax.broadcasted_iota(jnp.int32, (tq, tk), 1)
            keep = col <= row

        def scores(g):
            q = q_ref[:, 2 * g * LANES:(2 * g + 2) * LANES]
            k = k_ref[:, 2 * g * LANES:(2 * g + 2) * LANES]
            s = _dot_nt(q, k)
            if masked:
                s = jnp.where(keep, s, NEG_BIG)
            return s

        def update(g, s):
            v_ext = jnp.concatenate([v_ref[:, g * LANES:(g + 1) * LANES], ones], axis=1)
            rb = tq // ATTN_ROW_SPLIT
            for i in range(ATTN_ROW_SPLIT):
                rows = slice(i * rb, (i + 1) * rb)
                s_i = s[rows, :]
                m_prev = m_ref[g, rows, :]
                m_new = jnp.maximum(m_prev, jnp.max(s_i, axis=1, keepdims=True))
                alpha = jnp.exp2(m_prev - m_new)
                pr = jnp.exp2(s_i - jnp.concatenate([m_new] * (tk // LANES), axis=1))
                acc_ref[g, rows, :] = (jnp.concatenate([alpha, alpha], axis=1) * acc_ref[g, rows, :]
                                       + _dot(pr.astype(BF16), v_ext))
                m_ref[g, rows, :] = m_new

        s_prev = scores(0)
        for g in range(1, group):
            s_next = scores(g)
            update(g - 1, s_prev)
            s_prev = s_next
        update(group - 1, s_prev)

    crosses = k0 + tk - 1 > q0

    @pl.when(crosses)
    def _():
        step(True)

    @pl.when(jnp.logical_not(crosses))
    def _():
        step(False)

    @pl.when(k0 + tk >= q0 + tq)
    def _():
        for g in range(group):
            acc = acc_ref[g]
            out_ref[:, g * LANES:(g + 1) * LANES] = (acc[:, :LANES] / acc[:, LANES:]).astype(out_ref.dtype)


def _attention(q, k, v, *, batch, seq, heads):
    t = q.shape[0]
    tq = _tile(seq, TILES["attn_q"])
    tk = _tile(tq, TILES["attn_k"])
    group = _tile(heads, TILES["attn_heads"])
    nq, nk = seq // tq, seq // tk
    pairs = [(i, j) for i in range(nq) for j in range(((i + 1) * tq - 1) // tk + 1)]
    qi_tab = jnp.asarray([a for a, _ in pairs], jnp.int32)
    kj_tab = jnp.asarray([b for _, b in pairs], jnp.int32)

    grid_spec = pltpu.PrefetchScalarGridSpec(
        num_scalar_prefetch=2,
        grid=(batch, heads // group, len(pairs)),
        in_specs=[
            pl.BlockSpec((tq, 2 * group * LANES), lambda b, h, p, qi, kj: (b * nq + qi[p], h)),
            pl.BlockSpec((tk, 2 * group * LANES), lambda b, h, p, qi, kj: (b * nk + kj[p], h)),
            pl.BlockSpec((tk, group * LANES), lambda b, h, p, qi, kj: (b * nk + kj[p], h)),
        ],
        out_specs=pl.BlockSpec((tq, group * LANES), lambda b, h, p, qi, kj: (b * nq + qi[p], h)),
        scratch_shapes=[pltpu.VMEM((group, tq, LANES), F32), pltpu.VMEM((group, tq, 2 * LANES), F32)],
    )
    return pl.pallas_call(
        functools.partial(_attn_kernel, group=group, tq=tq, tk=tk),
        grid_spec=grid_spec,
        out_shape=jax.ShapeDtypeStruct((t, heads * LANES), BF16),
        compiler_params=_cparams(("parallel", "parallel", "arbitrary")),
        name="mla_attention",
    )(qi_tab, kj_tab, q, k, v)


def _pad_cols(w, n):
    return jnp.pad(w, ((0, 0), (0, n - w.shape[1])))


def _swap_halves(w):
    half = w.shape[-1] // 2
    return jnp.concatenate([w[..., half:], w[..., :half]], axis=-1)


def _pad_heads(w, heads):
    kdim = w.shape[0]
    w = w.reshape(kdim, heads, -1)
    w = jnp.pad(w, ((0, 0), (0, 0), (0, LANES - w.shape[-1])))
    return w.reshape(kdim, heads * LANES)


def kernel(x, positions, norm_mix, norm_mlp, gla_w_in, gla_w_gate_up, gla_b_gate, gla_norm, gla_w_out,
           kv_norm_in, mla_w_dkv, mla_kv_norm, mla_w_uk, mla_w_uv, mla_w_dq, mla_q_norm, mla_w_uq,
           mla_w_o, mlp_w1, mlp_w2, final_norm):
    batch, seq, d = x.shape
    t = batch * seq
    depth = norm_mix.shape[0]
    n_gla = gla_w_in.shape[0]

    gla_dk = gla_w_gate_up.shape[2]
    gla_dv = gla_w_out.shape[1]
    gla_dv_h = gla_norm.shape[1]
    gla_heads = gla_dv // gla_dv_h
    gla_dk_h = gla_dk // gla_heads
    n_main = 2 * gla_dk + 2 * gla_dv

    kv_lora = mla_kv_norm.shape[0]
    rope_dim = mla_w_dkv.shape[1] - kv_lora
    mla_heads = (mla_w_uq.shape[2] - mla_w_uk.shape[1]) // rope_dim
    nope_dim = mla_w_uk.shape[1] // mla_heads
    assert nope_dim == LANES and mla_w_uv.shape[1] == mla_heads * LANES and rope_dim <= LANES
    assert gla_dk_h % LANES == 0 and gla_dv_h % LANES == 0
    attn_scale = float(nope_dim + rope_dim) ** -0.5 * LOG2_E

    h = x.reshape(t, d)
    pos = positions.reshape(t, 1)
    row = lambda a: a.reshape(1, -1)

    cos_t = sin_t = k_cat = v_all = None
    for layer in range(depth):
        if layer < n_gla:
            a = layer
            w_in = gla_w_in[a]
            w_main = w_in[:, :n_main].astype(BF16)
            w_g = _pad_cols(w_in[:, n_main:], LANES).astype(BF16)
            w_up = jnp.pad(gla_w_gate_up[a], ((0, LANES - gla_w_gate_up.shape[1]), (0, 0))).astype(BF16)
            qkvr, log_a = _gla_inproj(h, row(norm_mix[layer]), w_main, w_g, w_up, row(gla_b_gate[a]))
            o = _gla(qkvr, log_a, row(gla_norm[a]), batch=batch, seq=seq, heads=gla_heads,
                     dk_h=gla_dk_h, dv_h=gla_dv_h)
            h = _outproj(h, o, gla_w_out[a].astype(BF16))
        else:
            b = layer - n_gla
            if layer == n_gla:
                cos_t, sin_t = _rope_tables(pos, rope_dim)
                w_lat = mla_w_dkv[:, :kv_lora]
                w_rope = mla_w_dkv[:, kv_lora:]
                w_dkv = jnp.concatenate(
                    [w_lat, _pad_cols(w_rope, LANES), _pad_cols(_swap_halves(w_rope), LANES)], axis=1)
                k_cat, v_all = _mla_kv(h, row(kv_norm_in), w_dkv.astype(BF16), row(mla_kv_norm),
                                       mla_w_uk.astype(BF16), mla_w_uv.astype(BF16), cos_t, sin_t,
                                       heads=mla_heads)
            w_uq = mla_w_uq[b].reshape(-1, mla_heads, nope_dim + rope_dim)
            q_lora = w_uq.shape[0]
            w_n = w_uq[:, :, :nope_dim].reshape(q_lora, mla_heads * nope_dim)
            w_r = w_uq[:, :, nope_dim:]
            w_rp = _pad_heads(w_r.reshape(q_lora, -1), mla_heads)
            w_sp = _pad_heads(_swap_halves(w_r).reshape(q_lora, -1), mla_heads)
            q_cat = _mla_q(h, row(norm_mix[layer]), mla_w_dq[b].astype(BF16), row(mla_q_norm[b]),
                           w_n.astype(BF16), w_rp.astype(BF16), w_sp.astype(BF16), cos_t, sin_t,
                           heads=mla_heads, scale=attn_scale)
            o = _attention(q_cat, k_cat, v_all, batch=batch, seq=seq, heads=mla_heads)
            h = _outproj(h, o, mla_w_o[b].astype(BF16))
        h = _mlp(h, row(norm_mlp[layer]), mlp_w1[layer].astype(BF16), mlp_w2[layer].astype(BF16),
                 row(final_norm), final_norm=(layer == depth - 1))
    return h.reshape(batch, seq, d)
```

```python
import functools

import jax
import jax.numpy as jnp
from jax import lax
from jax.experimental import pallas as pl
from jax.experimental.pallas import tpu as pltpu

BF16 = jnp.bfloat16
F32 = jnp.float32

RMS_EPS = 1e-6
ROPE_THETA = 10000.0
GLA_TAU = 16.0
GLA_CHUNK = 64
GLA_SUB = 8
LANES = 128
NEG_BIG = -1e30
LOG2_E = 1.4426950408889634
ATTN_ROW_SPLIT = 2
INPROJ_SPLIT = 4

VMEM_LIMIT_BYTES = 56 * 1024 * 1024

TILES = dict(
    inproj_m=1024, inproj_n=1024,
    gla_tokens=512,
    outproj_m=512,
    mlp_m=1024, mlp_f=512,
    kv_m=512,
    q_m=512,
    rope_m=2048,
    attn_q=1024, attn_k=1024, attn_heads=4,
)


def _cparams(sem):
    return pltpu.CompilerParams(dimension_semantics=sem, vmem_limit_bytes=VMEM_LIMIT_BYTES)


def _tile(n, t):
    t = min(n, t)
    assert n % t == 0, (n, t)
    return t


def _rms(x, g):
    return x * lax.rsqrt(jnp.mean(x * x, axis=-1, keepdims=True) + RMS_EPS) * g


def _dot(a, b):
    return jnp.dot(a, b, preferred_element_type=F32)


def _dot_nt(a, b):
    return lax.dot_general(a, b, (((1,), (1,)), ((), ())), preferred_element_type=F32)


def _dot_tn(a, b):
    return lax.dot_general(a, b, (((0,), (0,)), ((), ())), preferred_element_type=F32)


def _gla_inproj_kernel(x_ref, g_ref, w_ref, wg_ref, wup_ref, bg_ref, out_ref, la_ref, xn_ref):
    @pl.when(pl.program_id(1) == 0)
    def _():
        xn = _rms(x_ref[...], g_ref[...]).astype(BF16)
        xn_ref[...] = xn
        g_low = _dot(xn, wg_ref[...]).astype(BF16)
        tm, tn = out_ref.shape
        rb, cb = tm // INPROJ_SPLIT, tn // INPROJ_SPLIT
        for c in range(INPROJ_SPLIT):
            out_ref[:, c * cb:(c + 1) * cb] = _dot(xn, w_ref[:, c * cb:(c + 1) * cb]).astype(out_ref.dtype)
            z = _dot(g_low[c * rb:(c + 1) * rb, :], wup_ref[...]) + bg_ref[...]
            log_sig = jnp.minimum(z, 0.0) - jnp.log(1.0 + jnp.exp(-jnp.abs(z)))
            la_ref[c * rb:(c + 1) * rb, :] = log_sig * (LOG2_E / GLA_TAU)

    @pl.when(pl.program_id(1) != 0)
    def _():
        out_ref[...] = _dot(xn_ref[...], w_ref[...]).astype(out_ref.dtype)


def _gla_inproj(h, g, w, n, wg, wup, bg):
    t, d = h.shape
    dk = wup.shape[1]
    tm = _tile(t, TILES["inproj_m"])
    tn = _tile(n, TILES["inproj_n"])
    return pl.pallas_call(
        _gla_inproj_kernel,
        grid=(t // tm, n // tn),
        in_specs=[
            pl.BlockSpec((tm, d), lambda i, j: (i, 0)),
            pl.BlockSpec((1, d), lambda i, j: (0, 0)),
            pl.BlockSpec((d, tn), lambda i, j: (0, j)),
            pl.BlockSpec(wg.shape, lambda i, j: (0, 0)),
            pl.BlockSpec(wup.shape, lambda i, j: (0, 0)),
            pl.BlockSpec((1, dk), lambda i, j: (0, 0)),
        ],
        out_specs=[
            pl.BlockSpec((tm, tn), lambda i, j: (i, j)),
            pl.BlockSpec((tm, dk), lambda i, j: (i, 0)),
        ],
        out_shape=[
            jax.ShapeDtypeStruct((t, n), BF16),
            jax.ShapeDtypeStruct((t, dk), F32),
        ],
        scratch_shapes=[pltpu.VMEM((tm, d), BF16)],
        compiler_params=_cparams(("parallel", "arbitrary")),
        name="gla_inproj",
    )(h, g, w, wg, wup, bg)


def _gla_kernel(x_ref, la_ref, gn_ref, out_ref, st_ref, *, n_chunks, heads, dk_h, dv_h, scale):
    c_len, sub = GLA_CHUNK, GLA_SUB
    k_off = heads * dk_h
    v_off = 2 * heads * dk_h
    r_off = v_off + heads * dv_h

    @pl.when(pl.program_id(1) == 0)
    def _():
        st_ref[...] = jnp.zeros_like(st_ref)

    row = lax.broadcasted_iota(jnp.int32, (c_len, c_len), 0)
    col = lax.broadcasted_iota(jnp.int32, (c_len, c_len), 1)
    tril = (col <= row).astype(F32)
    causal = col <= row
    blk = 2 * sub
    n_blk = c_len // blk
    far = col < (row // blk) * blk
    near = jnp.logical_and(jnp.logical_and(row % blk >= sub, col // blk == row // blk), col % blk < sub)
    srow = lax.broadcasted_iota(jnp.int32, (sub * sub, c_len), 0)
    scol = lax.broadcasted_iota(jnp.int32, (sub * sub, c_len), 1)
    diag_base = scol - srow // sub
    gn = gn_ref[...]

    def load(h, rows):
        q = x_ref[rows, h * dk_h:(h + 1) * dk_h].astype(F32) * scale
        k = x_ref[rows, k_off + h * dk_h:k_off + (h + 1) * dk_h]
        v = x_ref[rows, v_off + h * dv_h:v_off + (h + 1) * dv_h]
        return q, k, k.astype(F32), v

    def inter(h, b, q, k_f, v):
        b_last = b[c_len - 1:c_len, :]
        st = st_ref[h]
        o = _dot_nt((q * jnp.exp2(b)).astype(BF16), st.astype(BF16))
        k_dec = (k_f * jnp.exp2(b_last - b)).astype(BF16)
        st_ref[h] = st * jnp.exp2(b_last) + _dot_tn(v, k_dec)
        return o

    def diag_scores(b, q, k):
        diag = []
        for s in range(c_len // sub):
            lo = s * sub
            b_s = b[lo:lo + sub, :]
            q_s = q[lo:lo + sub, :]
            stacked = jnp.concatenate(
                [q_s * jnp.exp2(b_s - b[lo + jj:lo + jj + 1, :]) for jj in range(sub)], axis=0)
            full = _dot_nt(stacked.astype(BF16), k)
            full = jnp.where(diag_base == lo, full, 0.0)
            diag.append(jnp.sum(full.reshape(sub, sub, c_len), axis=0))
        return jnp.concatenate(diag, axis=0)

    def near_scores(b, q, k_f):
        b_mid = jnp.concatenate(
            [jnp.broadcast_to(b[t * blk + sub:t * blk + sub + 1, :], (blk, dk_h)) for t in range(n_blk)], axis=0)
        return _dot_nt((q * jnp.exp2(b - b_mid)).astype(BF16), (k_f * jnp.exp2(b_mid - b)).astype(BF16))

    def far_scores(b, q, k_f):
        far_s = [jnp.zeros((blk, c_len), F32)]
        for t in range(1, n_blk):
            lo = t * blk
            b_ref = b[lo:lo + 1, :]
            q_t = (q[lo:lo + blk, :] * jnp.exp2(b[lo:lo + blk, :] - b_ref)).astype(BF16)
            k_t = jnp.concatenate([k_f[:lo, :] * jnp.exp2(b_ref - b[:lo, :]),
                                   jnp.zeros((c_len - lo, dk_h), F32)], axis=0).astype(BF16)
            far_s.append(_dot_nt(q_t, k_t))
        return jnp.concatenate(far_s, axis=0)

    def emit(h, rows, o, diag, near_s, far_s, v):
        scores = jnp.where(far, far_s, jnp.where(near, near_s, diag))
        scores = jnp.where(causal, scores, 0.0)
        o = _rms(o + _dot(scores.astype(BF16), v), gn)
        r = x_ref[rows, r_off + h * dv_h:r_off + (h + 1) * dv_h].astype(F32)
        out_ref[rows, h * dv_h:(h + 1) * dv_h] = (o * (r * jax.nn.sigmoid(r))).astype(out_ref.dtype)

    def cumsum(c):
        rows = pl.ds(pl.multiple_of(c * c_len, c_len), c_len)
        return jnp.dot(tril, la_ref[rows, :], precision=lax.Precision.HIGHEST,
                       preferred_element_type=F32)

    def chunk(c, b_all):
        b_next = cumsum(jnp.minimum(c + 1, n_chunks - 1))
        rows = pl.ds(pl.multiple_of(c * c_len, c_len), c_len)
        hs = range(heads)
        bs = [b_all[:, h * dk_h:(h + 1) * dk_h] for h in hs]
        qkv = [load(h, rows) for h in hs]
        o = [inter(h, bs[h], qkv[h][0], qkv[h][2], qkv[h][3]) for h in hs]
        prev = None
        for h in hs:
            dg = diag_scores(bs[h], qkv[h][0], qkv[h][1])
            nr = near_scores(bs[h], qkv[h][0], qkv[h][2])
            fr = far_scores(bs[h], qkv[h][0], qkv[h][2])
            if prev is not None:
                emit(*prev)
            prev = (h, rows, o[h], dg, nr, fr, qkv[h][3])
        emit(*prev)
        return b_next

    lax.fori_loop(0, n_chunks, chunk, cumsum(0))


def _gla(qkvr, log_a, gnorm, *, batch, seq, heads, dk_h, dv_h):
    t, n = qkvr.shape
    tt = _tile(seq, TILES["gla_tokens"])
    ns = seq // tt
    kern = functools.partial(_gla_kernel, n_chunks=tt // GLA_CHUNK, heads=heads, dk_h=dk_h, dv_h=dv_h,
                             scale=float(dk_h) ** -0.5)
    return pl.pallas_call(
        kern,
        grid=(batch, ns),
        in_specs=[
            pl.BlockSpec((tt, n), lambda b, i: (b * ns + i, 0)),
            pl.BlockSpec((tt, heads * dk_h), lambda b, i: (b * ns + i, 0)),
            pl.BlockSpec((1, dv_h), lambda b, i: (0, 0)),
        ],
        out_specs=pl.BlockSpec((tt, heads * dv_h), lambda b, i: (b * ns + i, 0)),
        out_shape=jax.ShapeDtypeStruct((t, heads * dv_h), BF16),
        scratch_shapes=[pltpu.VMEM((heads, dv_h, dk_h), F32)],
        compiler_params=_cparams(("parallel", "arbitrary")),
        name="gla_chunk",
    )(qkvr, log_a, gnorm)


def _outproj_kernel(h_ref, a_ref, w_ref, out_ref):
    out_ref[...] = h_ref[...] + _dot(a_ref[...], w_ref[...])


def _outproj(h, a, w):
    t, d = h.shape
    k = a.shape[1]
    tm = _tile(t, TILES["outproj_m"])
    return pl.pallas_call(
        _outproj_kernel,
        grid=(t // tm,),
        in_specs=[
            pl.BlockSpec((tm, d), lambda i: (i, 0)),
            pl.BlockSpec((tm, k), lambda i: (i, 0)),
            pl.BlockSpec((k, d), lambda i: (0, 0)),
        ],
        out_specs=pl.BlockSpec((tm, d), lambda i: (i, 0)),
        out_shape=jax.ShapeDtypeStruct((t, d), F32),
        compiler_params=_cparams(("parallel",)),
        name="outproj",
    )(h, a, w)


def _mlp_kernel(x_ref, g_ref, w1_ref, w2_ref, gf_ref, out_ref, xn_ref, hid_ref, *, final_norm):
    j = pl.program_id(1)
    last = pl.num_programs(1) - 1

    def up(slot):
        hid = _dot(xn_ref[...], w1_ref[...])
        hid_ref[slot] = jnp.square(jnp.maximum(hid, 0.0)).astype(BF16)

    def down(slot):
        out_ref[...] += _dot(hid_ref[slot], w2_ref[...])

    @pl.when(j == 0)
    def _():
        x = x_ref[...]
        xn_ref[...] = _rms(x, g_ref[...]).astype(BF16)
        out_ref[...] = x
        up(0)

    @pl.when(jnp.logical_and(j > 0, j < last))
    def _():
        down((j - 1) % 2)
        up(j % 2)

    @pl.when(j == last)
    def _():
        down((j - 1) % 2)
        if final_norm:
            out_ref[...] = _rms(out_ref[...], gf_ref[...])


def _mlp(h, g, w1, w2, gf, *, final_norm):
    t, d = h.shape
    f = w1.shape[1]
    tm = _tile(t, TILES["mlp_m"])
    tf = _tile(f, TILES["mlp_f"])
    nf = f // tf
    return pl.pallas_call(
        functools.partial(_mlp_kernel, final_norm=final_norm),
        grid=(t // tm, nf + 1),
        in_specs=[
            pl.BlockSpec((tm, d), lambda i, j: (i, 0)),
            pl.BlockSpec((1, d), lambda i, j: (0, 0)),
            pl.BlockSpec((d, tf), lambda i, j: (0, jnp.minimum(j, nf - 1))),
            pl.BlockSpec((tf, d), lambda i, j: (jnp.maximum(j - 1, 0), 0)),
            pl.BlockSpec((1, d), lambda i, j: (0, 0)),
        ],
        out_specs=pl.BlockSpec((tm, d), lambda i, j: (i, 0)),
        out_shape=jax.ShapeDtypeStruct((t, d), F32),
        scratch_shapes=[pltpu.VMEM((tm, d), BF16), pltpu.VMEM((2, tm, tf), BF16)],
        compiler_params=_cparams(("parallel", "arbitrary")),
        name="mlp",
    )(h, g, w1, w2, gf)


def _rope_table_kernel(pos_ref, freq_ref, sign_ref, keep_ref, cos_ref, sin_ref):
    ang = pos_ref[...].astype(F32) * freq_ref[...]
    cos_ref[...] = jnp.cos(ang) * keep_ref[...]
    sin_ref[...] = jnp.sin(ang) * sign_ref[...]


def _rope_tables(pos, rope_dim):
    t = pos.shape[0]
    half = rope_dim // 2
    freqs = ROPE_THETA ** (-jnp.arange(0, rope_dim, 2, dtype=F32) / rope_dim)
    pad = jnp.zeros((LANES - rope_dim,), F32)
    freq = jnp.concatenate([freqs, freqs, pad]).reshape(1, LANES)
    sign = jnp.concatenate([-jnp.ones((half,), F32), jnp.ones((half,), F32), pad]).reshape(1, LANES)
    keep = jnp.concatenate([jnp.ones((rope_dim,), F32), pad]).reshape(1, LANES)
    tm = _tile(t, TILES["rope_m"])
    row = pl.BlockSpec((1, LANES), lambda i: (0, 0))
    tab = pl.BlockSpec((tm, LANES), lambda i: (i, 0))
    return pl.pallas_call(
        _rope_table_kernel,
        grid=(t // tm,),
        in_specs=[pl.BlockSpec((tm, 1), lambda i: (i, 0)), row, row, row],
        out_specs=[tab, tab],
        out_shape=[jax.ShapeDtypeStruct((t, LANES), F32)] * 2,
        compiler_params=_cparams(("parallel",)),
        name="rope_tables",
    )(pos, freq, sign, keep)


def _mla_kv_kernel(x_ref, g_ref, wd_ref, gkv_ref, wuk_ref, wuv_ref, cos_ref, sin_ref,
                   k_ref, v_ref, *, heads, lora):
    xn = _rms(x_ref[...], g_ref[...]).astype(BF16)
    ckv = _dot(xn, wd_ref[...])
    c = _rms(ckv[:, :lora], gkv_ref[...]).astype(BF16)
    k_rope = ckv[:, lora:lora + LANES] * cos_ref[...] + ckv[:, lora + LANES:] * sin_ref[...]
    k_rope = k_rope.astype(BF16)
    k_nope = _dot(c, wuk_ref[...]).astype(BF16)
    v_ref[...] = _dot(c, wuv_ref[...]).astype(BF16)
    for h in range(heads):
        k_ref[:, 2 * h * LANES:(2 * h + 1) * LANES] = k_nope[:, h * LANES:(h + 1) * LANES]
        k_ref[:, (2 * h + 1) * LANES:(2 * h + 2) * LANES] = k_rope


def _mla_kv(h, g, wd, gkv, wuk, wuv, cos_t, sin_t, *, heads):
    t, d = h.shape
    lora = gkv.shape[1]
    tm = _tile(t, TILES["kv_m"])
    full = lambda a: pl.BlockSpec(a.shape, lambda i: (0, 0))
    return pl.pallas_call(
        functools.partial(_mla_kv_kernel, heads=heads, lora=lora),
        grid=(t // tm,),
        in_specs=[
            pl.BlockSpec((tm, d), lambda i: (i, 0)),
            full(g), full(wd), full(gkv), full(wuk), full(wuv),
            pl.BlockSpec((tm, LANES), lambda i: (i, 0)),
            pl.BlockSpec((tm, LANES), lambda i: (i, 0)),
        ],
        out_specs=[
            pl.BlockSpec((tm, 2 * heads * LANES), lambda i: (i, 0)),
            pl.BlockSpec((tm, heads * LANES), lambda i: (i, 0)),
        ],
        out_shape=[
            jax.ShapeDtypeStruct((t, 2 * heads * LANES), BF16),
            jax.ShapeDtypeStruct((t, heads * LANES), BF16),
        ],
        compiler_params=_cparams(("parallel",)),
        name="mla_kv",
    )(h, g, wd, gkv, wuk, wuv, cos_t, sin_t)


def _mla_q_kernel(x_ref, g_ref, wd_ref, gq_ref, wn_ref, wr_ref, ws_ref, cos_ref, sin_ref,
                  q_ref, *, heads, scale):
    xn = _rms(x_ref[...], g_ref[...]).astype(BF16)
    cq = _rms(_dot(xn, wd_ref[...]), gq_ref[...]).astype(BF16)
    q_nope = _dot(cq, wn_ref[...])
    q_rope = _dot(cq, wr_ref[...])
    q_swap = _dot(cq, ws_ref[...])
    cos_t = cos_ref[...]
    sin_t = sin_ref[...]
    for h in range(heads):
        sl = slice(h * LANES, (h + 1) * LANES)
        q_ref[:, 2 * h * LANES:(2 * h + 1) * LANES] = (q_nope[:, sl] * scale).astype(BF16)
        roped = q_rope[:, sl] * cos_t + q_swap[:, sl] * sin_t
        q_ref[:, (2 * h + 1) * LANES:(2 * h + 2) * LANES] = (roped * scale).astype(BF16)


def _mla_q(h, g, wd, gq, wn, wr, ws, cos_t, sin_t, *, heads, scale):
    t, d = h.shape
    tm = _tile(t, TILES["q_m"])
    full = lambda a: pl.BlockSpec(a.shape, lambda i: (0, 0))
    return pl.pallas_call(
        functools.partial(_mla_q_kernel, heads=heads, scale=scale),
        grid=(t // tm,),
        in_specs=[
            pl.BlockSpec((tm, d), lambda i: (i, 0)),
            full(g), full(wd), full(gq), full(wn), full(wr), full(ws),
            pl.BlockSpec((tm, LANES), lambda i: (i, 0)),
            pl.BlockSpec((tm, LANES), lambda i: (i, 0)),
        ],
        out_specs=pl.BlockSpec((tm, 2 * heads * LANES), lambda i: (i, 0)),
        out_shape=jax.ShapeDtypeStruct((t, 2 * heads * LANES), BF16),
        compiler_params=_cparams(("parallel",)),
        name="mla_q",
    )(h, g, wd, gq, wn, wr, ws, cos_t, sin_t)


def _attn_kernel(qi_ref, kj_ref, q_ref, k_ref, v_ref, out_ref, m_ref, acc_ref, *, group, tq, tk):
    p = pl.program_id(2)
    qi = qi_ref[p]
    kj = kj_ref[p]
    q0 = qi * tq
    k0 = kj * tk

    @pl.when(kj == 0)
    def _():
        m_ref[...] = jnp.full_like(m_ref, NEG_BIG)
        acc_ref[...] = jnp.zeros_like(acc_ref)

    ones = jnp.ones((tk, LANES), BF16)

    def step(masked):
        if masked:
            row = q0 + lax.broadcasted_iota(jnp.int32, (tq, tk), 0)
            col = k0 + lax.broadcasted_iota(jnp.int32, (tq, tk), 1)
            keep = col <= row

        def scores(g):
            q = q_ref[:, 2 * g * LANES:(2 * g + 2) * LANES]
            k = k_ref[:, 2 * g * LANES:(2 * g + 2) * LANES]
            s = _dot_nt(q, k)
            if masked:
                s = jnp.where(keep, s, NEG_BIG)
            return s

        def update(g, s):
            v_ext = jnp.concatenate([v_ref[:, g * LANES:(g + 1) * LANES], ones], axis=1)
            rb = tq // ATTN_ROW_SPLIT
            for i in range(ATTN_ROW_SPLIT):
                rows = slice(i * rb, (i + 1) * rb)
                s_i = s[rows, :]
                m_prev = m_ref[g, rows, :]
                m_new = jnp.maximum(m_prev, jnp.max(s_i, axis=1, keepdims=True))
                alpha = jnp.exp2(m_prev - m_new)
                pr = jnp.exp2(s_i - jnp.concatenate([m_new] * (tk // LANES), axis=1))
                acc_ref[g, rows, :] = (jnp.concatenate([alpha, alpha], axis=1) * acc_ref[g, rows, :]
                                       + _dot(pr.astype(BF16), v_ext))
                m_ref[g, rows, :] = m_new

        s_prev = scores(0)
        for g in range(1, group):
            s_next = scores(g)
            update(g - 1, s_prev)
            s_prev = s_next
        update(group - 1, s_prev)

    crosses = k0 + tk - 1 > q0

    @pl.when(crosses)
    def _():
        step(True)

    @pl.when(jnp.logical_not(crosses))
    def _():
        step(False)

    @pl.when(k0 + tk >= q0 + tq)
    def _():
        for g in range(group):
            acc = acc_ref[g]
            out_ref[:, g * LANES:(g + 1) * LANES] = (acc[:, :LANES] / acc[:, LANES:]).astype(out_ref.dtype)


def _attention(q, k, v, *, batch, seq, heads):
    t = q.shape[0]
    tq = _tile(seq, TILES["attn_q"])
    tk = _tile(tq, TILES["attn_k"])
    group = _tile(heads, TILES["attn_heads"])
    nq, nk = seq // tq, seq // tk
    pairs = [(i, j) for i in range(nq) for j in range(((i + 1) * tq - 1) // tk + 1)]
    qi_tab = jnp.asarray([a for a, _ in pairs], jnp.int32)
    kj_tab = jnp.asarray([b for _, b in pairs], jnp.int32)

    grid_spec = pltpu.PrefetchScalarGridSpec(
        num_scalar_prefetch=2,
        grid=(batch, heads // group, len(pairs)),
        in_specs=[
            pl.BlockSpec((tq, 2 * group * LANES), lambda b, h, p, qi, kj: (b * nq + qi[p], h)),
            pl.BlockSpec((tk, 2 * group * LANES), lambda b, h, p, qi, kj: (b * nk + kj[p], h)),
            pl.BlockSpec((tk, group * LANES), lambda b, h, p, qi, kj: (b * nk + kj[p], h)),
        ],
        out_specs=pl.BlockSpec((tq, group * LANES), lambda b, h, p, qi, kj: (b * nq + qi[p], h)),
        scratch_shapes=[pltpu.VMEM((group, tq, LANES), F32), pltpu.VMEM((group, tq, 2 * LANES), F32)],
    )
    return pl.pallas_call(
        functools.partial(_attn_kernel, group=group, tq=tq, tk=tk),
        grid_spec=grid_spec,
        out_shape=jax.ShapeDtypeStruct((t, heads * LANES), BF16),
        compiler_params=_cparams(("parallel", "parallel", "arbitrary")),
        name="mla_attention",
    )(qi_tab, kj_tab, q, k, v)


def _pad_cols(w, n):
    return jnp.pad(w, ((0, 0), (0, n - w.shape[1])))


def _swap_halves(w):
    half = w.shape[-1] // 2
    return jnp.concatenate([w[..., half:], w[..., :half]], axis=-1)


def _pad_heads(w, heads):
    kdim = w.shape[0]
    w = w.reshape(kdim, heads, -1)
    w = jnp.pad(w, ((0, 0), (0, 0), (0, LANES - w.shape[-1])))
    return w.reshape(kdim, heads * LANES)


def kernel(x, positions, norm_mix, norm_mlp, gla_w_in, gla_w_gate_up, gla_b_gate, gla_norm, gla_w_out,
           kv_norm_in, mla_w_dkv, mla_kv_norm, mla_w_uk, mla_w_uv, mla_w_dq, mla_q_norm, mla_w_uq,
           mla_w_o, mlp_w1, mlp_w2, final_norm):
    batch, seq, d = x.shape
    t = batch * seq
    depth = norm_mix.shape[0]
    n_gla = gla_w_in.shape[0]

    gla_dk = gla_w_gate_up.shape[2]
    gla_dv = gla_w_out.shape[1]
    gla_dv_h = gla_norm.shape[1]
    gla_heads = gla_dv // gla_dv_h
    gla_dk_h = gla_dk // gla_heads
    n_main = 2 * gla_dk + 2 * gla_dv

    kv_lora = mla_kv_norm.shape[0]
    rope_dim = mla_w_dkv.shape[1] - kv_lora
    mla_heads = (mla_w_uq.shape[2] - mla_w_uk.shape[1]) // rope_dim
    nope_dim = mla_w_uk.shape[1] // mla_heads
    assert nope_dim == LANES and mla_w_uv.shape[1] == mla_heads * LANES and rope_dim <= LANES
    assert gla_dk_h % LANES == 0 and gla_dv_h % LANES == 0
    attn_scale = float(nope_dim + rope_dim) ** -0.5 * LOG2_E

    h = x.reshape(t, d)
    pos = positions.reshape(t, 1)
    row = lambda a: a.reshape(1, -1)

    gla_w_in_bf = gla_w_in.astype(BF16)
    cos_t = sin_t = k_cat = v_all = None
    for layer in range(depth):
        if layer < n_gla:
            a = layer
            w_g = _pad_cols(gla_w_in[a][:, n_main:], LANES).astype(BF16)
            w_up = jnp.pad(gla_w_gate_up[a], ((0, LANES - gla_w_gate_up.shape[1]), (0, 0))).astype(BF16)
            qkvr, log_a = _gla_inproj(h, row(norm_mix[layer]), gla_w_in_bf[a], n_main, w_g, w_up,
                                      row(gla_b_gate[a]))
            o = _gla(qkvr, log_a, row(gla_norm[a]), batch=batch, seq=seq, heads=gla_heads,
                     dk_h=gla_dk_h, dv_h=gla_dv_h)
            h = _outproj(h, o, gla_w_out[a].astype(BF16))
        else:
            b = layer - n_gla
            if layer == n_gla:
                cos_t, sin_t = _rope_tables(pos, rope_dim)
                w_lat = mla_w_dkv[:, :kv_lora]
                w_rope = mla_w_dkv[:, kv_lora:]
                w_dkv = jnp.concatenate(
                    [w_lat, _pad_cols(w_rope, LANES), _pad_cols(_swap_halves(w_rope), LANES)], axis=1)
                k_cat, v_all = _mla_kv(h, row(kv_norm_in), w_dkv.astype(BF16), row(mla_kv_norm),
                                       mla_w_uk.astype(BF16), mla_w_uv.astype(BF16), cos_t, sin_t,
                                       heads=mla_heads)
            w_uq = mla_w_uq[b].reshape(-1, mla_heads, nope_dim + rope_dim)
            q_lora = w_uq.shape[0]
            w_n = w_uq[:, :, :nope_dim].reshape(q_lora, mla_heads * nope_dim)
            w_r = w_uq[:, :, nope_dim:]
            w_rp = _pad_heads(w_r.reshape(q_lora, -1), mla_heads)
            w_sp = _pad_heads(_swap_halves(w_r).reshape(q_lora, -1), mla_heads)
            q_cat = _mla_q(h, row(norm_mix[layer]), mla_w_dq[b].astype(BF16), row(mla_q_norm[b]),
                           w_n.astype(BF16), w_rp.astype(BF16), w_sp.astype(BF16), cos_t, sin_t,
                           heads=mla_heads, scale=attn_scale)
            o = _attention(q_cat, k_cat, v_all, batch=batch, seq=seq, heads=mla_heads)
            h = _outproj(h, o, mla_w_o[b].astype(BF16))
        h = _mlp(h, row(norm_mlp[layer]), mlp_w1[layer].astype(BF16), mlp_w2[layer].astype(BF16),
                 row(final_norm), final_norm=(layer == depth - 1))
    return h.reshape(batch, seq, d)
```

```python
import functools

import jax
import jax.numpy as jnp
from jax import lax
from jax.experimental import pallas as pl
from jax.experimental.pallas import tpu as pltpu

BF16 = jnp.bfloat16
F32 = jnp.float32

RMS_EPS = 1e-6
ROPE_THETA = 10000.0
GLA_TAU = 16.0
GLA_CHUNK = 64
GLA_SUB = 8
LANES = 128
NEG_BIG = -1e30
LOG2_E = 1.4426950408889634
ATTN_ROW_SPLIT = 2
INPROJ_SPLIT = 4

VMEM_LIMIT_BYTES = 56 * 1024 * 1024

TILES = dict(
    inproj_m=1024, inproj_n=1024,
    gla_tokens=512,
    outproj_m=512,
    mlp_m=1024, mlp_f=512,
    kv_m=512,
    q_m=512,
    rope_m=2048,
    attn_q=1024, attn_k=1024, attn_heads=4,
)


def _cparams(sem):
    return pltpu.CompilerParams(dimension_semantics=sem, vmem_limit_bytes=VMEM_LIMIT_BYTES)


def _tile(n, t):
    t = min(n, t)
    assert n % t == 0, (n, t)
    return t


def _rms(x, g):
    return x * lax.rsqrt(jnp.mean(x * x, axis=-1, keepdims=True) + RMS_EPS) * g


def _dot(a, b):
    return jnp.dot(a, b, preferred_element_type=F32)


def _dot_nt(a, b):
    return lax.dot_general(a, b, (((1,), (1,)), ((), ())), preferred_element_type=F32)


def _dot_tn(a, b):
    return lax.dot_general(a, b, (((0,), (0,)), ((), ())), preferred_element_type=F32)


def _gla_inproj_kernel(x_ref, g_ref, w_ref, wg_ref, wup_ref, bg_ref, out_ref, la_ref, xn_ref, *, rank):
    @pl.when(pl.program_id(1) == 0)
    def _():
        xn = _rms(x_ref[...], g_ref[...]).astype(BF16)
        xn_ref[...] = xn
        lane = lax.broadcasted_iota(jnp.int32, wg_ref.shape, 1)
        w_gate = jnp.where(lane < rank, wg_ref[...], 0.0).astype(BF16)
        g_low = _dot(xn, w_gate).astype(BF16)
        tm, tn = out_ref.shape
        rb, cb = tm // INPROJ_SPLIT, tn // INPROJ_SPLIT
        for c in range(INPROJ_SPLIT):
            out_ref[:, c * cb:(c + 1) * cb] = _dot(xn, w_ref[:, c * cb:(c + 1) * cb]).astype(out_ref.dtype)
            z = _dot(g_low[c * rb:(c + 1) * rb, :], wup_ref[...]) + bg_ref[...]
            log_sig = jnp.minimum(z, 0.0) - jnp.log(1.0 + jnp.exp(-jnp.abs(z)))
            la_ref[c * rb:(c + 1) * rb, :] = log_sig * (LOG2_E / GLA_TAU)

    @pl.when(pl.program_id(1) != 0)
    def _():
        out_ref[...] = _dot(xn_ref[...], w_ref[...]).astype(out_ref.dtype)


def _gla_inproj(h, g, w, w_full, wup, bg):
    t, d = h.shape
    n = w.shape[1]
    dk = wup.shape[1]
    rank = w_full.shape[1] - n
    assert n % LANES == 0 and rank <= LANES
    tm = _tile(t, TILES["inproj_m"])
    tn = _tile(n, TILES["inproj_n"])
    return pl.pallas_call(
        functools.partial(_gla_inproj_kernel, rank=rank),
        grid=(t // tm, n // tn),
        in_specs=[
            pl.BlockSpec((tm, d), lambda i, j: (i, 0)),
            pl.BlockSpec((1, d), lambda i, j: (0, 0)),
            pl.BlockSpec((d, tn), lambda i, j: (0, j)),
            pl.BlockSpec((d, LANES), lambda i, j: (0, n // LANES)),
            pl.BlockSpec(wup.shape, lambda i, j: (0, 0)),
            pl.BlockSpec((1, dk), lambda i, j: (0, 0)),
        ],
        out_specs=[
            pl.BlockSpec((tm, tn), lambda i, j: (i, j)),
            pl.BlockSpec((tm, dk), lambda i, j: (i, 0)),
        ],
        out_shape=[
            jax.ShapeDtypeStruct((t, n), BF16),
            jax.ShapeDtypeStruct((t, dk), F32),
        ],
        scratch_shapes=[pltpu.VMEM((tm, d), BF16)],
        compiler_params=_cparams(("parallel", "arbitrary")),
        name="gla_inproj",
    )(h, g, w, w_full, wup, bg)


def _gla_kernel(x_ref, la_ref, gn_ref, out_ref, st_ref, *, n_chunks, heads, dk_h, dv_h, scale):
    c_len, sub = GLA_CHUNK, GLA_SUB
    k_off = heads * dk_h
    v_off = 2 * heads * dk_h
    r_off = v_off + heads * dv_h

    @pl.when(pl.program_id(1) == 0)
    def _():
        st_ref[...] = jnp.zeros_like(st_ref)

    row = lax.broadcasted_iota(jnp.int32, (c_len, c_len), 0)
    col = lax.broadcasted_iota(jnp.int32, (c_len, c_len), 1)
    tril = (col <= row).astype(F32)
    causal = col <= row
    blk = 2 * sub
    n_blk = c_len // blk
    far = col < (row // blk) * blk
    near = jnp.logical_and(jnp.logical_and(row % blk >= sub, col // blk == row // blk), col % blk < sub)
    srow = lax.broadcasted_iota(jnp.int32, (sub * sub, c_len), 0)
    scol = lax.broadcasted_iota(jnp.int32, (sub * sub, c_len), 1)
    diag_base = scol - srow // sub
    gn = gn_ref[...]

    def load(h, rows):
        q = x_ref[rows, h * dk_h:(h + 1) * dk_h].astype(F32) * scale
        k = x_ref[rows, k_off + h * dk_h:k_off + (h + 1) * dk_h]
        v = x_ref[rows, v_off + h * dv_h:v_off + (h + 1) * dv_h]
        return q, k, k.astype(F32), v

    def inter(h, b, q, k_f, v):
        b_last = b[c_len - 1:c_len, :]
        st = st_ref[h]
        o = _dot_nt((q * jnp.exp2(b)).astype(BF16), st.astype(BF16))
        k_dec = (k_f * jnp.exp2(b_last - b)).astype(BF16)
        st_ref[h] = st * jnp.exp2(b_last) + _dot_tn(v, k_dec)
        return o

    def diag_scores(b, q, k):
        diag = []
        for s in range(c_len // sub):
            lo = s * sub
            b_s = b[lo:lo + sub, :]
            q_s = q[lo:lo + sub, :]
            stacked = jnp.concatenate(
                [q_s * jnp.exp2(b_s - b[lo + jj:lo + jj + 1, :]) for jj in range(sub)], axis=0)
            full = _dot_nt(stacked.astype(BF16), k)
            full = jnp.where(diag_base == lo, full, 0.0)
            diag.append(jnp.sum(full.reshape(sub, sub, c_len), axis=0))
        return jnp.concatenate(diag, axis=0)

    def near_scores(b, q, k_f):
        b_mid = jnp.concatenate(
            [jnp.broadcast_to(b[t * blk + sub:t * blk + sub + 1, :], (blk, dk_h)) for t in range(n_blk)], axis=0)
        return _dot_nt((q * jnp.exp2(b - b_mid)).astype(BF16), (k_f * jnp.exp2(b_mid - b)).astype(BF16))

    def far_scores(b, q, k_f):
        far_s = [jnp.zeros((blk, c_len), F32)]
        for t in range(1, n_blk):
            lo = t * blk
            b_ref = b[lo:lo + 1, :]
            q_t = (q[lo:lo + blk, :] * jnp.exp2(b[lo:lo + blk, :] - b_ref)).astype(BF16)
            k_t = jnp.concatenate([k_f[:lo, :] * jnp.exp2(b_ref - b[:lo, :]),
                                   jnp.zeros((c_len - lo, dk_h), F32)], axis=0).astype(BF16)
            far_s.append(_dot_nt(q_t, k_t))
        return jnp.concatenate(far_s, axis=0)

    def emit(h, rows, o, diag, near_s, far_s, v):
        scores = jnp.where(far, far_s, jnp.where(near, near_s, diag))
        scores = jnp.where(causal, scores, 0.0)
        o = _rms(o + _dot(scores.astype(BF16), v), gn)
        r = x_ref[rows, r_off + h * dv_h:r_off + (h + 1) * dv_h].astype(F32)
        out_ref[rows, h * dv_h:(h + 1) * dv_h] = (o * (r * jax.nn.sigmoid(r))).astype(out_ref.dtype)

    def cumsum(c):
        rows = pl.ds(pl.multiple_of(c * c_len, c_len), c_len)
        return jnp.dot(tril, la_ref[rows, :], precision=lax.Precision.HIGHEST,
                       preferred_element_type=F32)

    def chunk(c, b_all):
        b_next = cumsum(jnp.minimum(c + 1, n_chunks - 1))
        rows = pl.ds(pl.multiple_of(c * c_len, c_len), c_len)
        hs = range(heads)
        bs = [b_all[:, h * dk_h:(h + 1) * dk_h] for h in hs]
        qkv = [load(h, rows) for h in hs]
        o = [inter(h, bs[h], qkv[h][0], qkv[h][2], qkv[h][3]) for h in hs]
        prev = None
        for h in hs:
            dg = diag_scores(bs[h], qkv[h][0], qkv[h][1])
            nr = near_scores(bs[h], qkv[h][0], qkv[h][2])
            fr = far_scores(bs[h], qkv[h][0], qkv[h][2])
            if prev is not None:
                emit(*prev)
            prev = (h, rows, o[h], dg, nr, fr, qkv[h][3])
        emit(*prev)
        return b_next

    lax.fori_loop(0, n_chunks, chunk, cumsum(0))


def _gla(qkvr, log_a, gnorm, *, batch, seq, heads, dk_h, dv_h):
    t, n = qkvr.shape
    tt = _tile(seq, TILES["gla_tokens"])
    ns = seq // tt
    kern = functools.partial(_gla_kernel, n_chunks=tt // GLA_CHUNK, heads=heads, dk_h=dk_h, dv_h=dv_h,
                             scale=float(dk_h) ** -0.5)
    return pl.pallas_call(
        kern,
        grid=(batch, ns),
        in_specs=[
            pl.BlockSpec((tt, n), lambda b, i: (b * ns + i, 0)),
            pl.BlockSpec((tt, heads * dk_h), lambda b, i: (b * ns + i, 0)),
            pl.BlockSpec((1, dv_h), lambda b, i: (0, 0)),
        ],
        out_specs=pl.BlockSpec((tt, heads * dv_h), lambda b, i: (b * ns + i, 0)),
        out_shape=jax.ShapeDtypeStruct((t, heads * dv_h), BF16),
        scratch_shapes=[pltpu.VMEM((heads, dv_h, dk_h), F32)],
        compiler_params=_cparams(("parallel", "arbitrary")),
        name="gla_chunk",
    )(qkvr, log_a, gnorm)


def _outproj_kernel(h_ref, a_ref, w_ref, out_ref):
    out_ref[...] = h_ref[...] + _dot(a_ref[...], w_ref[...])


def _outproj(h, a, w):
    t, d = h.shape
    k = a.shape[1]
    tm = _tile(t, TILES["outproj_m"])
    return pl.pallas_call(
        _outproj_kernel,
        grid=(t // tm,),
        in_specs=[
            pl.BlockSpec((tm, d), lambda i: (i, 0)),
            pl.BlockSpec((tm, k), lambda i: (i, 0)),
            pl.BlockSpec((k, d), lambda i: (0, 0)),
        ],
        out_specs=pl.BlockSpec((tm, d), lambda i: (i, 0)),
        out_shape=jax.ShapeDtypeStruct((t, d), F32),
        compiler_params=_cparams(("parallel",)),
        name="outproj",
    )(h, a, w)


def _mlp_kernel(x_ref, g_ref, w1_ref, w2_ref, gf_ref, out_ref, xn_ref, *, final_norm):
    j = pl.program_id(1)

    @pl.when(j == 0)
    def _():
        x = x_ref[...]
        xn_ref[...] = _rms(x, g_ref[...]).astype(BF16)
        out_ref[...] = x

    hid = _dot(xn_ref[...], w1_ref[...])
    hid = jnp.square(jnp.maximum(hid, 0.0)).astype(BF16)
    out_ref[...] += _dot(hid, w2_ref[...])

    if final_norm:
        @pl.when(j == pl.num_programs(1) - 1)
        def _():
            out_ref[...] = _rms(out_ref[...], gf_ref[...])


def _mlp(h, g, w1, w2, gf, *, final_norm):
    t, d = h.shape
    f = w1.shape[1]
    tm = _tile(t, TILES["mlp_m"])
    tf = _tile(f, TILES["mlp_f"])
    return pl.pallas_call(
        functools.partial(_mlp_kernel, final_norm=final_norm),
        grid=(t // tm, f // tf),
        in_specs=[
            pl.BlockSpec((tm, d), lambda i, j: (i, 0)),
            pl.BlockSpec((1, d), lambda i, j: (0, 0)),
            pl.BlockSpec((d, tf), lambda i, j: (0, j)),
            pl.BlockSpec((tf, d), lambda i, j: (j, 0)),
            pl.BlockSpec((1, d), lambda i, j: (0, 0)),
        ],
        out_specs=pl.BlockSpec((tm, d), lambda i, j: (i, 0)),
        out_shape=jax.ShapeDtypeStruct((t, d), F32),
        scratch_shapes=[pltpu.VMEM((tm, d), BF16)],
        compiler_params=_cparams(("parallel", "arbitrary")),
        name="mlp",
    )(h, g, w1, w2, gf)


def _rope_table_kernel(pos_ref, freq_ref, sign_ref, keep_ref, cos_ref, sin_ref):
    ang = pos_ref[...].astype(F32) * freq_ref[...]
    cos_ref[...] = jnp.cos(ang) * keep_ref[...]
    sin_ref[...] = jnp.sin(ang) * sign_ref[...]


def _rope_tables(pos, rope_dim):
    t = pos.shape[0]
    half = rope_dim // 2
    freqs = ROPE_THETA ** (-jnp.arange(0, rope_dim, 2, dtype=F32) / rope_dim)
    pad = jnp.zeros((LANES - rope_dim,), F32)
    freq = jnp.concatenate([freqs, freqs, pad]).reshape(1, LANES)
    sign = jnp.concatenate([-jnp.ones((half,), F32), jnp.ones((half,), F32), pad]).reshape(1, LANES)
    keep = jnp.concatenate([jnp.ones((rope_dim,), F32), pad]).reshape(1, LANES)
    tm = _tile(t, TILES["rope_m"])
    row = pl.BlockSpec((1, LANES), lambda i: (0, 0))
    tab = pl.BlockSpec((tm, LANES), lambda i: (i, 0))
    return pl.pallas_call(
        _rope_table_kernel,
        grid=(t // tm,),
        in_specs=[pl.BlockSpec((tm, 1), lambda i: (i, 0)), row, row, row],
        out_specs=[tab, tab],
        out_shape=[jax.ShapeDtypeStruct((t, LANES), F32)] * 2,
        compiler_params=_cparams(("parallel",)),
        name="rope_tables",
    )(pos, freq, sign, keep)


def _mla_kv_kernel(x_ref, g_ref, wd_ref, gkv_ref, wuk_ref, wuv_ref, cos_ref, sin_ref,
                   k_ref, v_ref, *, heads, lora):
    xn = _rms(x_ref[...], g_ref[...]).astype(BF16)
    ckv = _dot(xn, wd_ref[...])
    c = _rms(ckv[:, :lora], gkv_ref[...]).astype(BF16)
    k_rope = ckv[:, lora:lora + LANES] * cos_ref[...] + ckv[:, lora + LANES:] * sin_ref[...]
    k_rope = k_rope.astype(BF16)
    k_nope = _dot(c, wuk_ref[...]).astype(BF16)
    v_ref[...] = _dot(c, wuv_ref[...]).astype(BF16)
    for h in range(heads):
        k_ref[:, 2 * h * LANES:(2 * h + 1) * LANES] = k_nope[:, h * LANES:(h + 1) * LANES]
        k_ref[:, (2 * h + 1) * LANES:(2 * h + 2) * LANES] = k_rope


def _mla_kv(h, g, wd, gkv, wuk, wuv, cos_t, sin_t, *, heads):
    t, d = h.shape
    lora = gkv.shape[1]
    tm = _tile(t, TILES["kv_m"])
    full = lambda a: pl.BlockSpec(a.shape, lambda i: (0, 0))
    return pl.pallas_call(
        functools.partial(_mla_kv_kernel, heads=heads, lora=lora),
        grid=(t // tm,),
        in_specs=[
            pl.BlockSpec((tm, d), lambda i: (i, 0)),
            full(g), full(wd), full(gkv), full(wuk), full(wuv),
            pl.BlockSpec((tm, LANES), lambda i: (i, 0)),
            pl.BlockSpec((tm, LANES), lambda i: (i, 0)),
        ],
        out_specs=[
            pl.BlockSpec((tm, 2 * heads * LANES), lambda i: (i, 0)),
            pl.BlockSpec((tm, heads * LANES), lambda i: (i, 0)),
        ],
        out_shape=[
            jax.ShapeDtypeStruct((t, 2 * heads * LANES), BF16),
            jax.ShapeDtypeStruct((t, heads * LANES), BF16),
        ],
        compiler_params=_cparams(("parallel",)),
        name="mla_kv",
    )(h, g, wd, gkv, wuk, wuv, cos_t, sin_t)


def _mla_q_kernel(x_ref, g_ref, wd_ref, gq_ref, wn_ref, wr_ref, ws_ref, cos_ref, sin_ref,
                  q_ref, *, heads, scale):
    xn = _rms(x_ref[...], g_ref[...]).astype(BF16)
    cq = _rms(_dot(xn, wd_ref[...]), gq_ref[...]).astype(BF16)
    q_nope = _dot(cq, wn_ref[...])
    q_rope = _dot(cq, wr_ref[...])
    q_swap = _dot(cq, ws_ref[...])
    cos_t = cos_ref[...]
    sin_t = sin_ref[...]
    for h in range(heads):
        sl = slice(h * LANES, (h + 1) * LANES)
        q_ref[:, 2 * h * LANES:(2 * h + 1) * LANES] = (q_nope[:, sl] * scale).astype(BF16)
        roped = q_rope[:, sl] * cos_t + q_swap[:, sl] * sin_t
        q_ref[:, (2 * h + 1) * LANES:(2 * h + 2) * LANES] = (roped * scale).astype(BF16)


def _mla_q(h, g, wd, gq, wn, wr, ws, cos_t, sin_t, *, heads, scale):
    t, d = h.shape
    tm = _tile(t, TILES["q_m"])
    full = lambda a: pl.BlockSpec(a.shape, lambda i: (0, 0))
    return pl.pallas_call(
        functools.partial(_mla_q_kernel, heads=heads, scale=scale),
        grid=(t // tm,),
        in_specs=[
            pl.BlockSpec((tm, d), lambda i: (i, 0)),
            full(g), full(wd), full(gq), full(wn), full(wr), full(ws),
            pl.BlockSpec((tm, LANES), lambda i: (i, 0)),
            pl.BlockSpec((tm, LANES), lambda i: (i, 0)),
        ],
        out_specs=pl.BlockSpec((tm, 2 * heads * LANES), lambda i: (i, 0)),
        out_shape=jax.ShapeDtypeStruct((t, 2 * heads * LANES), BF16),
        compiler_params=_cparams(("parallel",)),
        name="mla_q",
    )(h, g, wd, gq, wn, wr, ws, cos_t, sin_t)


def _attn_kernel(qi_ref, kj_ref, q_ref, k_ref, v_ref, out_ref, m_ref, acc_ref, *, group, tq, tk):
    p = pl.program_id(2)
    qi = qi_ref[p]
    kj = kj_ref[p]
    q0 = qi * tq
    k0 = kj * tk

    @pl.when(kj == 0)
    def _():
        m_ref[...] = jnp.full_like(m_ref, NEG_BIG)
        acc_ref[...] = jnp.zeros_like(acc_ref)

    ones = jnp.ones((tk, LANES), BF16)

    def step(masked):
        if masked:
            row = q0 + lax.broadcasted_iota(jnp.int32, (tq, tk), 0)
            col = k0 + lax.broadcasted_iota(jnp.int32, (tq, tk), 1)
            keep = col <= row

        def scores(g):
            q = q_ref[:, 2 * g * LANES:(2 * g + 2) * LANES]
            k = k_ref[:, 2 * g * LANES:(2 * g + 2) * LANES]
            s = _dot_nt(q, k)
            if masked:
                s = jnp.where(keep, s, NEG_BIG)
            return s

        def update(g, s):
            v_ext = jnp.concatenate([v_ref[:, g * LANES:(g + 1) * LANES], ones], axis=1)
            rb = tq // ATTN_ROW_SPLIT
            for i in range(ATTN_ROW_SPLIT):
                rows = slice(i * rb, (i + 1) * rb)
                kw = min(tk, (i + 1) * rb) if (masked and tq == tk) else tk
                s_i = s[rows, :kw]
                m_prev = m_ref[g, rows, :]
                m_new = jnp.maximum(m_prev, jnp.max(s_i, axis=1, keepdims=True))
                alpha = jnp.exp2(m_prev - m_new)
                pr = jnp.exp2(s_i - jnp.concatenate([m_new] * (kw // LANES), axis=1))
                acc_ref[g, rows, :] = (jnp.concatenate([alpha, alpha], axis=1) * acc_ref[g, rows, :]
                                       + _dot(pr.astype(BF16), v_ext[:kw, :]))
                m_ref[g, rows, :] = m_new

        s_prev = scores(0)
        for g in range(1, group):
            s_next = scores(g)
            update(g - 1, s_prev)
            s_prev = s_next
        update(group - 1, s_prev)

    crosses = k0 + tk - 1 > q0

    @pl.when(crosses)
    def _():
        step(True)

    @pl.when(jnp.logical_not(crosses))
    def _():
        step(False)

    @pl.when(k0 + tk >= q0 + tq)
    def _():
        for g in range(group):
            acc = acc_ref[g]
            out_ref[:, g * LANES:(g + 1) * LANES] = (acc[:, :LANES] / acc[:, LANES:]).astype(out_ref.dtype)


def _attention(q, k, v, *, batch, seq, heads):
    t = q.shape[0]
    tq = _tile(seq, TILES["attn_q"])
    tk = _tile(tq, TILES["attn_k"])
    group = _tile(heads, TILES["attn_heads"])
    nq, nk = seq // tq, seq // tk
    pairs = [(i, j) for i in range(nq) for j in range(((i + 1) * tq - 1) // tk + 1)]
    qi_tab = jnp.asarray([a for a, _ in pairs], jnp.int32)
    kj_tab = jnp.asarray([b for _, b in pairs], jnp.int32)

    grid_spec = pltpu.PrefetchScalarGridSpec(
        num_scalar_prefetch=2,
        grid=(batch, heads // group, len(pairs)),
        in_specs=[
            pl.BlockSpec((tq, 2 * group * LANES), lambda b, h, p, qi, kj: (b * nq + qi[p], h)),
            pl.BlockSpec((tk, 2 * group * LANES), lambda b, h, p, qi, kj: (b * nk + kj[p], h)),
            pl.BlockSpec((tk, group * LANES), lambda b, h, p, qi, kj: (b * nk + kj[p], h)),
        ],
        out_specs=pl.BlockSpec((tq, group * LANES), lambda b, h, p, qi, kj: (b * nq + qi[p], h)),
        scratch_shapes=[pltpu.VMEM((group, tq, LANES), F32), pltpu.VMEM((group, tq, 2 * LANES), F32)],
    )
    return pl.pallas_call(
        functools.partial(_attn_kernel, group=group, tq=tq, tk=tk),
        grid_spec=grid_spec,
        out_shape=jax.ShapeDtypeStruct((t, heads * LANES), BF16),
        compiler_params=_cparams(("parallel", "parallel", "arbitrary")),
        name="mla_attention",
    )(qi_tab, kj_tab, q, k, v)


def _pad_cols(w, n):
    return jnp.pad(w, ((0, 0), (0, n - w.shape[1])))


def _swap_halves(w):
    half = w.shape[-1] // 2
    return jnp.concatenate([w[..., half:], w[..., :half]], axis=-1)


def _pad_heads(w, heads):
    kdim = w.shape[0]
    w = w.reshape(kdim, heads, -1)
    w = jnp.pad(w, ((0, 0), (0, 0), (0, LANES - w.shape[-1])))
    return w.reshape(kdim, heads * LANES)


def kernel(x, positions, norm_mix, norm_mlp, gla_w_in, gla_w_gate_up, gla_b_gate, gla_norm, gla_w_out,
           kv_norm_in, mla_w_dkv, mla_kv_norm, mla_w_uk, mla_w_uv, mla_w_dq, mla_q_norm, mla_w_uq,
           mla_w_o, mlp_w1, mlp_w2, final_norm):
    batch, seq, d = x.shape
    t = batch * seq
    depth = norm_mix.shape[0]
    n_gla = gla_w_in.shape[0]

    gla_dk = gla_w_gate_up.shape[2]
    gla_dv = gla_w_out.shape[1]
    gla_dv_h = gla_norm.shape[1]
    gla_heads = gla_dv // gla_dv_h
    gla_dk_h = gla_dk // gla_heads
    n_main = 2 * gla_dk + 2 * gla_dv

    kv_lora = mla_kv_norm.shape[0]
    rope_dim = mla_w_dkv.shape[1] - kv_lora
    mla_heads = (mla_w_uq.shape[2] - mla_w_uk.shape[1]) // rope_dim
    nope_dim = mla_w_uk.shape[1] // mla_heads
    assert nope_dim == LANES and mla_w_uv.shape[1] == mla_heads * LANES and rope_dim <= LANES
    assert gla_dk_h % LANES == 0 and gla_dv_h % LANES == 0
    attn_scale = float(nope_dim + rope_dim) ** -0.5 * LOG2_E

    h = x.reshape(t, d)
    pos = positions.reshape(t, 1)
    row = lambda a: a.reshape(1, -1)

    cos_t = sin_t = k_cat = v_all = None
    for layer in range(depth):
        if layer < n_gla:
            a = layer
            w_in = gla_w_in[a]
            w_main = w_in[:, :n_main].astype(BF16)
            w_up = jnp.pad(gla_w_gate_up[a], ((0, LANES - gla_w_gate_up.shape[1]), (0, 0))).astype(BF16)
            qkvr, log_a = _gla_inproj(h, row(norm_mix[layer]), w_main, w_in, w_up, row(gla_b_gate[a]))
            o = _gla(qkvr, log_a, row(gla_norm[a]), batch=batch, seq=seq, heads=gla_heads,
                     dk_h=gla_dk_h, dv_h=gla_dv_h)
            h = _outproj(h, o, gla_w_out[a].astype(BF16))
        else:
            b = layer - n_gla
            if layer == n_gla:
                cos_t, sin_t = _rope_tables(pos, rope_dim)
                w_lat = mla_w_dkv[:, :kv_lora]
                w_rope = mla_w_dkv[:, kv_lora:]
                w_dkv = jnp.concatenate(
                    [w_lat, _pad_cols(w_rope, LANES), _pad_cols(_swap_halves(w_rope), LANES)], axis=1)
                k_cat, v_all = _mla_kv(h, row(kv_norm_in), w_dkv.astype(BF16), row(mla_kv_norm),
                                       mla_w_uk.astype(BF16), mla_w_uv.astype(BF16), cos_t, sin_t,
                                       heads=mla_heads)
            w_uq = mla_w_uq[b].reshape(-1, mla_heads, nope_dim + rope_dim)
            q_lora = w_uq.shape[0]
            w_n = w_uq[:, :, :nope_dim].reshape(q_lora, mla_heads * nope_dim)
            w_r = w_uq[:, :, nope_dim:]
            w_rp = _pad_heads(w_r.reshape(q_lora, -1), mla_heads)
            w_sp = _pad_heads(_swap_halves(w_r).reshape(q_lora, -1), mla_heads)
            q_cat = _mla_q(h, row(norm_mix[layer]), mla_w_dq[b].astype(BF16), row(mla_q_norm[b]),
                           w_n.astype(BF16), w_rp.astype(BF16), w_sp.astype(BF16), cos_t, sin_t,
                           heads=mla_heads, scale=attn_scale)
            o = _attention(q_cat, k_cat, v_all, batch=batch, seq=seq, heads=mla_heads)
            h = _outproj(h, o, mla_w_o[b].astype(BF16))
        h = _mlp(h, row(norm_mlp[layer]), mlp_w1[layer].astype(BF16), mlp_w2[layer].astype(BF16),
                 row(final_norm), final_norm=(layer == depth - 1))
    return h.reshape(batch, seq, d)
```

```python
import functools

import jax
import jax.numpy as jnp
from jax import lax
from jax.experimental import pallas as pl
from jax.experimental.pallas import tpu as pltpu

BF16 = jnp.bfloat16
F32 = jnp.float32

RMS_EPS = 1e-6
ROPE_THETA = 10000.0
GLA_TAU = 16.0
GLA_CHUNK = 64
GLA_SUB = 8
LANES = 128
NEG_BIG = -1e30
LOG2_E = 1.4426950408889634
ATTN_ROW_SPLIT = 2
ATTN_DIAG_ROW_SPLIT = 4
INPROJ_SPLIT = 4

VMEM_LIMIT_BYTES = 56 * 1024 * 1024

TILES = dict(
    inproj_m=1024, inproj_n=1024, cast_n=1024,
    gla_tokens=512,
    outproj_m=512,
    mlp_m=1024, mlp_f=512,
    kv_m=512,
    q_m=512,
    rope_m=2048,
    attn_q=1024, attn_k=1024, attn_heads=4,
)


def _cparams(sem):
    return pltpu.CompilerParams(dimension_semantics=sem, vmem_limit_bytes=VMEM_LIMIT_BYTES)


def _tile(n, t):
    t = min(n, t)
    assert n % t == 0, (n, t)
    return t


def _rms(x, g):
    return x * lax.rsqrt(jnp.mean(x * x, axis=-1, keepdims=True) + RMS_EPS) * g


def _dot(a, b):
    return jnp.dot(a, b, preferred_element_type=F32)


def _dot_nt(a, b):
    return lax.dot_general(a, b, (((1,), (1,)), ((), ())), preferred_element_type=F32)


def _dot_tn(a, b):
    return lax.dot_general(a, b, (((0,), (0,)), ((), ())), preferred_element_type=F32)


def _gla_inproj_kernel(x_ref, g_ref, w_ref, wg_ref, wup_ref, bg_ref, out_ref, la_ref, xn_ref, *, rank):
    @pl.when(pl.program_id(1) == 0)
    def _():
        xn = _rms(x_ref[...], g_ref[...]).astype(BF16)
        xn_ref[...] = xn
        lane = lax.broadcasted_iota(jnp.int32, wg_ref.shape, 1)
        w_gate = jnp.where(lane < rank, wg_ref[...], 0.0).astype(BF16)
        g_low = _dot(xn, w_gate).astype(BF16)
        tm, tn = out_ref.shape
        rb, cb = tm // INPROJ_SPLIT, tn // INPROJ_SPLIT
        for c in range(INPROJ_SPLIT):
            out_ref[:, c * cb:(c + 1) * cb] = _dot(xn, w_ref[:, c * cb:(c + 1) * cb]).astype(out_ref.dtype)
            z = _dot(g_low[c * rb:(c + 1) * rb, :], wup_ref[...]) + bg_ref[...]
            log_sig = jnp.minimum(z, 0.0) - jnp.log(1.0 + jnp.exp(-jnp.abs(z)))
            la_ref[c * rb:(c + 1) * rb, :] = log_sig * (LOG2_E / GLA_TAU)

    @pl.when(pl.program_id(1) != 0)
    def _():
        out_ref[...] = _dot(xn_ref[...], w_ref[...]).astype(out_ref.dtype)


def _gla_inproj(h, g, w, w_full, wup, bg):
    t, d = h.shape
    n = w.shape[1]
    dk = wup.shape[1]
    rank = w_full.shape[1] - n
    assert n % LANES == 0 and rank <= LANES
    tm = _tile(t, TILES["inproj_m"])
    tn = _tile(n, TILES["inproj_n"])
    return pl.pallas_call(
        functools.partial(_gla_inproj_kernel, rank=rank),
        grid=(t // tm, n // tn),
        in_specs=[
            pl.BlockSpec((tm, d), lambda i, j: (i, 0)),
            pl.BlockSpec((1, d), lambda i, j: (0, 0)),
            pl.BlockSpec((d, tn), lambda i, j: (0, j)),
            pl.BlockSpec((d, LANES), lambda i, j: (0, n // LANES)),
            pl.BlockSpec(wup.shape, lambda i, j: (0, 0)),
            pl.BlockSpec((1, dk), lambda i, j: (0, 0)),
        ],
        out_specs=[
            pl.BlockSpec((tm, tn), lambda i, j: (i, j)),
            pl.BlockSpec((tm, dk), lambda i, j: (i, 0)),
        ],
        out_shape=[
            jax.ShapeDtypeStruct((t, n), BF16),
            jax.ShapeDtypeStruct((t, dk), F32),
        ],
        scratch_shapes=[pltpu.VMEM((tm, d), BF16)],
        compiler_params=_cparams(("parallel", "arbitrary")),
        name="gla_inproj",
    )(h, g, w, w_full, wup, bg)


def _cast_cols_kernel(w_ref, out_ref):
    out_ref[...] = w_ref[...].astype(out_ref.dtype)


def _cast_cols(w, n):
    d = w.shape[0]
    tn = _tile(n, TILES["cast_n"])
    return pl.pallas_call(
        _cast_cols_kernel,
        grid=(n // tn,),
        in_specs=[pl.BlockSpec((d, tn), lambda j: (0, j))],
        out_specs=pl.BlockSpec((d, tn), lambda j: (0, j)),
        out_shape=jax.ShapeDtypeStruct((d, n), BF16),
        compiler_params=_cparams(("parallel",)),
        name="cast_cols",
    )(w)


def _gla_kernel(x_ref, la_ref, gn_ref, out_ref, st_ref, *, n_chunks, heads, dk_h, dv_h, scale):
    c_len, sub = GLA_CHUNK, GLA_SUB
    k_off = heads * dk_h
    v_off = 2 * heads * dk_h
    r_off = v_off + heads * dv_h

    @pl.when(pl.program_id(1) == 0)
    def _():
        st_ref[...] = jnp.zeros_like(st_ref)

    row = lax.broadcasted_iota(jnp.int32, (c_len, c_len), 0)
    col = lax.broadcasted_iota(jnp.int32, (c_len, c_len), 1)
    tril = (col <= row).astype(F32)
    causal = col <= row
    blk = 2 * sub
    n_blk = c_len // blk
    far = col < (row // blk) * blk
    near = jnp.logical_and(jnp.logical_and(row % blk >= sub, col // blk == row // blk), col % blk < sub)
    srow = lax.broadcasted_iota(jnp.int32, (sub * sub, c_len), 0)
    scol = lax.broadcasted_iota(jnp.int32, (sub * sub, c_len), 1)
    diag_base = scol - srow // sub
    gn = gn_ref[...]

    def load(h, rows):
        q = x_ref[rows, h * dk_h:(h + 1) * dk_h].astype(F32) * scale
        k = x_ref[rows, k_off + h * dk_h:k_off + (h + 1) * dk_h]
        v = x_ref[rows, v_off + h * dv_h:v_off + (h + 1) * dv_h]
        return q, k, k.astype(F32), v

    def inter(h, b, q, k_f, v):
        b_last = b[c_len - 1:c_len, :]
        st = st_ref[h]
        o = _dot_nt((q * jnp.exp2(b)).astype(BF16), st.astype(BF16))
        k_dec = (k_f * jnp.exp2(b_last - b)).astype(BF16)
        st_ref[h] = st * jnp.exp2(b_last) + _dot_tn(v, k_dec)
        return o

    def diag_scores(b, q, k):
        diag = []
        for s in range(c_len // sub):
            lo = s * sub
            b_s = b[lo:lo + sub, :]
            q_s = q[lo:lo + sub, :]
            stacked = jnp.concatenate(
                [q_s * jnp.exp2(b_s - b[lo + jj:lo + jj + 1, :]) for jj in range(sub)], axis=0)
            full = _dot_nt(stacked.astype(BF16), k)
            full = jnp.where(diag_base == lo, full, 0.0)
            diag.append(jnp.sum(full.reshape(sub, sub, c_len), axis=0))
        return jnp.concatenate(diag, axis=0)

    def near_scores(b, q, k_f):
        b_mid = jnp.concatenate(
            [jnp.broadcast_to(b[t * blk + sub:t * blk + sub + 1, :], (blk, dk_h)) for t in range(n_blk)], axis=0)
        return _dot_nt((q * jnp.exp2(b - b_mid)).astype(BF16), (k_f * jnp.exp2(b_mid - b)).astype(BF16))

    def far_scores(b, q, k_f):
        far_s = [jnp.zeros((blk, c_len), F32)]
        for t in range(1, n_blk):
            lo = t * blk
            b_ref = b[lo:lo + 1, :]
            q_t = (q[lo:lo + blk, :] * jnp.exp2(b[lo:lo + blk, :] - b_ref)).astype(BF16)
            k_t = jnp.concatenate([k_f[:lo, :] * jnp.exp2(b_ref - b[:lo, :]),
                                   jnp.zeros((c_len - lo, dk_h), F32)], axis=0).astype(BF16)
            far_s.append(_dot_nt(q_t, k_t))
        return jnp.concatenate(far_s, axis=0)

    def emit(h, rows, o, diag, near_s, far_s, v):
        scores = jnp.where(far, far_s, jnp.where(near, near_s, diag))
        scores = jnp.where(causal, scores, 0.0)
        o = _rms(o + _dot(scores.astype(BF16), v), gn)
        r = x_ref[rows, r_off + h * dv_h:r_off + (h + 1) * dv_h].astype(F32)
        out_ref[rows, h * dv_h:(h + 1) * dv_h] = (o * (r * jax.nn.sigmoid(r))).astype(out_ref.dtype)

    def cumsum(c):
        rows = pl.ds(pl.multiple_of(c * c_len, c_len), c_len)
        return jnp.dot(tril, la_ref[rows, :], precision=lax.Precision.HIGHEST,
                       preferred_element_type=F32)

    def chunk(c, b_all):
        b_next = cumsum(jnp.minimum(c + 1, n_chunks - 1))
        rows = pl.ds(pl.multiple_of(c * c_len, c_len), c_len)
        hs = range(heads)
        bs = [b_all[:, h * dk_h:(h + 1) * dk_h] for h in hs]
        qkv = [load(h, rows) for h in hs]
        o = [inter(h, bs[h], qkv[h][0], qkv[h][2], qkv[h][3]) for h in hs]
        prev = None
        for h in hs:
            dg = diag_scores(bs[h], qkv[h][0], qkv[h][1])
            nr = near_scores(bs[h], qkv[h][0], qkv[h][2])
            fr = far_scores(bs[h], qkv[h][0], qkv[h][2])
            if prev is not None:
                emit(*prev)
            prev = (h, rows, o[h], dg, nr, fr, qkv[h][3])
        emit(*prev)
        return b_next

    lax.fori_loop(0, n_chunks, chunk, cumsum(0))


def _gla(qkvr, log_a, gnorm, *, batch, seq, heads, dk_h, dv_h):
    t, n = qkvr.shape
    tt = _tile(seq, TILES["gla_tokens"])
    ns = seq // tt
    kern = functools.partial(_gla_kernel, n_chunks=tt // GLA_CHUNK, heads=heads, dk_h=dk_h, dv_h=dv_h,
                             scale=float(dk_h) ** -0.5)
    return pl.pallas_call(
        kern,
        grid=(batch, ns),
        in_specs=[
            pl.BlockSpec((tt, n), lambda b, i: (b * ns + i, 0)),
            pl.BlockSpec((tt, heads * dk_h), lambda b, i: (b * ns + i, 0)),
            pl.BlockSpec((1, dv_h), lambda b, i: (0, 0)),
        ],
        out_specs=pl.BlockSpec((tt, heads * dv_h), lambda b, i: (b * ns + i, 0)),
        out_shape=jax.ShapeDtypeStruct((t, heads * dv_h), BF16),
        scratch_shapes=[pltpu.VMEM((heads, dv_h, dk_h), F32)],
        compiler_params=_cparams(("parallel", "arbitrary")),
        name="gla_chunk",
    )(qkvr, log_a, gnorm)


def _outproj_kernel(h_ref, a_ref, w_ref, out_ref):
    out_ref[...] = h_ref[...] + _dot(a_ref[...], w_ref[...])


def _outproj(h, a, w):
    t, d = h.shape
    k = a.shape[1]
    tm = _tile(t, TILES["outproj_m"])
    return pl.pallas_call(
        _outproj_kernel,
        grid=(t // tm,),
        in_specs=[
            pl.BlockSpec((tm, d), lambda i: (i, 0)),
            pl.BlockSpec((tm, k), lambda i: (i, 0)),
            pl.BlockSpec((k, d), lambda i: (0, 0)),
        ],
        out_specs=pl.BlockSpec((tm, d), lambda i: (i, 0)),
        out_shape=jax.ShapeDtypeStruct((t, d), F32),
        compiler_params=_cparams(("parallel",)),
        name="outproj",
    )(h, a, w)


def _mlp_kernel(x_ref, g_ref, w1_ref, w2_ref, gf_ref, out_ref, xn_ref, *, final_norm):
    j = pl.program_id(1)

    @pl.when(j == 0)
    def _():
        x = x_ref[...]
        xn_ref[...] = _rms(x, g_ref[...]).astype(BF16)
        out_ref[...] = x

    hid = _dot(xn_ref[...], w1_ref[...])
    hid = jnp.square(jnp.maximum(hid, 0.0)).astype(BF16)
    out_ref[...] += _dot(hid, w2_ref[...])

    if final_norm:
        @pl.when(j == pl.num_programs(1) - 1)
        def _():
            out_ref[...] = _rms(out_ref[...], gf_ref[...])


def _mlp(h, g, w1, w2, gf, *, final_norm):
    t, d = h.shape
    f = w1.shape[1]
    tm = _tile(t, TILES["mlp_m"])
    tf = _tile(f, TILES["mlp_f"])
    return pl.pallas_call(
        functools.partial(_mlp_kernel, final_norm=final_norm),
        grid=(t // tm, f // tf),
        in_specs=[
            pl.BlockSpec((tm, d), lambda i, j: (i, 0)),
            pl.BlockSpec((1, d), lambda i, j: (0, 0)),
            pl.BlockSpec((d, tf), lambda i, j: (0, j)),
            pl.BlockSpec((tf, d), lambda i, j: (j, 0)),
            pl.BlockSpec((1, d), lambda i, j: (0, 0)),
        ],
        out_specs=pl.BlockSpec((tm, d), lambda i, j: (i, 0)),
        out_shape=jax.ShapeDtypeStruct((t, d), F32),
        scratch_shapes=[pltpu.VMEM((tm, d), BF16)],
        compiler_params=_cparams(("parallel", "arbitrary")),
        name="mlp",
    )(h, g, w1, w2, gf)


def _rope_table_kernel(pos_ref, freq_ref, sign_ref, keep_ref, cos_ref, sin_ref):
    ang = pos_ref[...].astype(F32) * freq_ref[...]
    cos_ref[...] = jnp.cos(ang) * keep_ref[...]
    sin_ref[...] = jnp.sin(ang) * sign_ref[...]


def _rope_tables(pos, rope_dim):
    t = pos.shape[0]
    half = rope_dim // 2
    freqs = ROPE_THETA ** (-jnp.arange(0, rope_dim, 2, dtype=F32) / rope_dim)
    pad = jnp.zeros((LANES - rope_dim,), F32)
    freq = jnp.concatenate([freqs, freqs, pad]).reshape(1, LANES)
    sign = jnp.concatenate([-jnp.ones((half,), F32), jnp.ones((half,), F32), pad]).reshape(1, LANES)
    keep = jnp.concatenate([jnp.ones((rope_dim,), F32), pad]).reshape(1, LANES)
    tm = _tile(t, TILES["rope_m"])
    row = pl.BlockSpec((1, LANES), lambda i: (0, 0))
    tab = pl.BlockSpec((tm, LANES), lambda i: (i, 0))
    return pl.pallas_call(
        _rope_table_kernel,
        grid=(t // tm,),
        in_specs=[pl.BlockSpec((tm, 1), lambda i: (i, 0)), row, row, row],
        out_specs=[tab, tab],
        out_shape=[jax.ShapeDtypeStruct((t, LANES), F32)] * 2,
        compiler_params=_cparams(("parallel",)),
        name="rope_tables",
    )(pos, freq, sign, keep)


def _mla_kv_kernel(x_ref, g_ref, wd_ref, gkv_ref, wuk_ref, wuv_ref, cos_ref, sin_ref,
                   k_ref, v_ref, *, heads, lora):
    xn = _rms(x_ref[...], g_ref[...]).astype(BF16)
    ckv = _dot(xn, wd_ref[...])
    c = _rms(ckv[:, :lora], gkv_ref[...]).astype(BF16)
    k_rope = ckv[:, lora:lora + LANES] * cos_ref[...] + ckv[:, lora + LANES:] * sin_ref[...]
    k_rope = k_rope.astype(BF16)
    k_nope = _dot(c, wuk_ref[...]).astype(BF16)
    v_ref[...] = _dot(c, wuv_ref[...]).astype(BF16)
    for h in range(heads):
        k_ref[:, 2 * h * LANES:(2 * h + 1) * LANES] = k_nope[:, h * LANES:(h + 1) * LANES]
        k_ref[:, (2 * h + 1) * LANES:(2 * h + 2) * LANES] = k_rope


def _mla_kv(h, g, wd, gkv, wuk, wuv, cos_t, sin_t, *, heads):
    t, d = h.shape
    lora = gkv.shape[1]
    tm = _tile(t, TILES["kv_m"])
    full = lambda a: pl.BlockSpec(a.shape, lambda i: (0, 0))
    return pl.pallas_call(
        functools.partial(_mla_kv_kernel, heads=heads, lora=lora),
        grid=(t // tm,),
        in_specs=[
            pl.BlockSpec((tm, d), lambda i: (i, 0)),
            full(g), full(wd), full(gkv), full(wuk), full(wuv),
            pl.BlockSpec((tm, LANES), lambda i: (i, 0)),
            pl.BlockSpec((tm, LANES), lambda i: (i, 0)),
        ],
        out_specs=[
            pl.BlockSpec((tm, 2 * heads * LANES), lambda i: (i, 0)),
            pl.BlockSpec((tm, heads * LANES), lambda i: (i, 0)),
        ],
        out_shape=[
            jax.ShapeDtypeStruct((t, 2 * heads * LANES), BF16),
            jax.ShapeDtypeStruct((t, heads * LANES), BF16),
        ],
        compiler_params=_cparams(("parallel",)),
        name="mla_kv",
    )(h, g, wd, gkv, wuk, wuv, cos_t, sin_t)


def _mla_q_kernel(x_ref, g_ref, wd_ref, gq_ref, wn_ref, wr_ref, ws_ref, cos_ref, sin_ref,
                  q_ref, *, heads, scale):
    xn = _rms(x_ref[...], g_ref[...]).astype(BF16)
    cq = _rms(_dot(xn, wd_ref[...]), gq_ref[...]).astype(BF16)
    q_nope = _dot(cq, wn_ref[...])
    q_rope = _dot(cq, wr_ref[...])
    q_swap = _dot(cq, ws_ref[...])
    cos_t = cos_ref[...]
    sin_t = sin_ref[...]
    for h in range(heads):
        sl = slice(h * LANES, (h + 1) * LANES)
        q_ref[:, 2 * h * LANES:(2 * h + 1) * LANES] = (q_nope[:, sl] * scale).astype(BF16)
        roped = q_rope[:, sl] * cos_t + q_swap[:, sl] * sin_t
        q_ref[:, (2 * h + 1) * LANES:(2 * h + 2) * LANES] = (roped * scale).astype(BF16)


def _mla_q(h, g, wd, gq, wn, wr, ws, cos_t, sin_t, *, heads, scale):
    t, d = h.shape
    tm = _tile(t, TILES["q_m"])
    full = lambda a: pl.BlockSpec(a.shape, lambda i: (0, 0))
    return pl.pallas_call(
        functools.partial(_mla_q_kernel, heads=heads, scale=scale),
        grid=(t // tm,),
        in_specs=[
            pl.BlockSpec((tm, d), lambda i: (i, 0)),
            full(g), full(wd), full(gq), full(wn), full(wr), full(ws),
            pl.BlockSpec((tm, LANES), lambda i: (i, 0)),
            pl.BlockSpec((tm, LANES), lambda i: (i, 0)),
        ],
        out_specs=pl.BlockSpec((tm, 2 * heads * LANES), lambda i: (i, 0)),
        out_shape=jax.ShapeDtypeStruct((t, 2 * heads * LANES), BF16),
        compiler_params=_cparams(("parallel",)),
        name="mla_q",
    )(h, g, wd, gq, wn, wr, ws, cos_t, sin_t)


def _attn_kernel(qi_ref, kj_ref, q_ref, k_ref, v_ref, out_ref, m_ref, acc_ref, *, group, tq, tk):
    p = pl.program_id(2)
    qi = qi_ref[p]
    kj = kj_ref[p]
    q0 = qi * tq
    k0 = kj * tk

    @pl.when(kj == 0)
    def _():
        m_ref[...] = jnp.full_like(m_ref, NEG_BIG)
        acc_ref[...] = jnp.zeros_like(acc_ref)

    ones = jnp.ones((tk, LANES), BF16)

    def step(masked):
        if masked:
            row = q0 + lax.broadcasted_iota(jnp.int32, (tq, tk), 0)
            col = k0 + lax.broadcasted_iota(jnp.int32, (tq, tk), 1)
            keep = col <= row

        def scores(g):
            q = q_ref[:, 2 * g * LANES:(2 * g + 2) * LANES]
            k = k_ref[:, 2 * g * LANES:(2 * g + 2) * LANES]
            s = _dot_nt(q, k)
            if masked:
                s = jnp.where(keep, s, NEG_BIG)
            return s

        def update(g, s):
            v_ext = jnp.concatenate([v_ref[:, g * LANES:(g + 1) * LANES], ones], axis=1)
            n_rows = ATTN_DIAG_ROW_SPLIT if masked else ATTN_ROW_SPLIT
            rb = tq // n_rows
            for i in range(n_rows):
                rows = slice(i * rb, (i + 1) * rb)
                kw = min(tk, -(-(i + 1) * rb // LANES) * LANES) if (masked and tq == tk) else tk
                s_i = s[rows, :kw]
                m_prev = m_ref[g, rows, :]
                m_new = jnp.maximum(m_prev, jnp.max(s_i, axis=1, keepdims=True))
                alpha = jnp.exp2(m_prev - m_new)
                pr = jnp.exp2(s_i - jnp.concatenate([m_new] * (kw // LANES), axis=1))
                acc_ref[g, rows, :] = (jnp.concatenate([alpha, alpha], axis=1) * acc_ref[g, rows, :]
                                       + _dot(pr.astype(BF16), v_ext[:kw, :]))
                m_ref[g, rows, :] = m_new

        s_prev = scores(0)
        for g in range(1, group):
            s_next = scores(g)
            update(g - 1, s_prev)
            s_prev = s_next
        update(group - 1, s_prev)

    crosses = k0 + tk - 1 > q0

    @pl.when(crosses)
    def _():
        step(True)

    @pl.when(jnp.logical_not(crosses))
    def _():
        step(False)

    @pl.when(k0 + tk >= q0 + tq)
    def _():
        for g in range(group):
            acc = acc_ref[g]
            out_ref[:, g * LANES:(g + 1) * LANES] = (acc[:, :LANES] / acc[:, LANES:]).astype(out_ref.dtype)


def _attention(q, k, v, *, batch, seq, heads):
    t = q.shape[0]
    tq = _tile(seq, TILES["attn_q"])
    tk = _tile(tq, TILES["attn_k"])
    group = _tile(heads, TILES["attn_heads"])
    nq, nk = seq // tq, seq // tk
    pairs = [(i, j) for i in range(nq) for j in range(((i + 1) * tq - 1) // tk + 1)]
    qi_tab = jnp.asarray([a for a, _ in pairs], jnp.int32)
    kj_tab = jnp.asarray([b for _, b in pairs], jnp.int32)

    grid_spec = pltpu.PrefetchScalarGridSpec(
        num_scalar_prefetch=2,
        grid=(batch, heads // group, len(pairs)),
        in_specs=[
            pl.BlockSpec((tq, 2 * group * LANES), lambda b, h, p, qi, kj: (b * nq + qi[p], h)),
            pl.BlockSpec((tk, 2 * group * LANES), lambda b, h, p, qi, kj: (b * nk + kj[p], h)),
            pl.BlockSpec((tk, group * LANES), lambda b, h, p, qi, kj: (b * nk + kj[p], h)),
        ],
        out_specs=pl.BlockSpec((tq, group * LANES), lambda b, h, p, qi, kj: (b * nq + qi[p], h)),
        scratch_shapes=[pltpu.VMEM((group, tq, LANES), F32), pltpu.VMEM((group, tq, 2 * LANES), F32)],
    )
    return pl.pallas_call(
        functools.partial(_attn_kernel, group=group, tq=tq, tk=tk),
        grid_spec=grid_spec,
        out_shape=jax.ShapeDtypeStruct((t, heads * LANES), BF16),
        compiler_params=_cparams(("parallel", "parallel", "arbitrary")),
        name="mla_attention",
    )(qi_tab, kj_tab, q, k, v)


def _pad_cols(w, n):
    return jnp.pad(w, ((0, 0), (0, n - w.shape[1])))


def _swap_halves(w):
    half = w.shape[-1] // 2
    return jnp.concatenate([w[..., half:], w[..., :half]], axis=-1)


def _pad_heads(w, heads):
    kdim = w.shape[0]
    w = w.reshape(kdim, heads, -1)
    w = jnp.pad(w, ((0, 0), (0, 0), (0, LANES - w.shape[-1])))
    return w.reshape(kdim, heads * LANES)


def kernel(x, positions, norm_mix, norm_mlp, gla_w_in, gla_w_gate_up, gla_b_gate, gla_norm, gla_w_out,
           kv_norm_in, mla_w_dkv, mla_kv_norm, mla_w_uk, mla_w_uv, mla_w_dq, mla_q_norm, mla_w_uq,
           mla_w_o, mlp_w1, mlp_w2, final_norm):
    batch, seq, d = x.shape
    t = batch * seq
    depth = norm_mix.shape[0]
    n_gla = gla_w_in.shape[0]

    gla_dk = gla_w_gate_up.shape[2]
    gla_dv = gla_w_out.shape[1]
    gla_dv_h = gla_norm.shape[1]
    gla_heads = gla_dv // gla_dv_h
    gla_dk_h = gla_dk // gla_heads
    n_main = 2 * gla_dk + 2 * gla_dv

    kv_lora = mla_kv_norm.shape[0]
    rope_dim = mla_w_dkv.shape[1] - kv_lora
    mla_heads = (mla_w_uq.shape[2] - mla_w_uk.shape[1]) // rope_dim
    nope_dim = mla_w_uk.shape[1] // mla_heads
    assert nope_dim == LANES and mla_w_uv.shape[1] == mla_heads * LANES and rope_dim <= LANES
    assert gla_dk_h % LANES == 0 and gla_dv_h % LANES == 0
    attn_scale = float(nope_dim + rope_dim) ** -0.5 * LOG2_E

    h = x.reshape(t, d)
    pos = positions.reshape(t, 1)
    row = lambda a: a.reshape(1, -1)

    cos_t = sin_t = k_cat = v_all = None
    for layer in range(depth):
        if layer < n_gla:
            a = layer
            w_in = gla_w_in[a]
            w_main = _cast_cols(w_in, n_main)
            w_up = jnp.pad(gla_w_gate_up[a], ((0, LANES - gla_w_gate_up.shape[1]), (0, 0))).astype(BF16)
            qkvr, log_a = _gla_inproj(h, row(norm_mix[layer]), w_main, w_in, w_up, row(gla_b_gate[a]))
            o = _gla(qkvr, log_a, row(gla_norm[a]), batch=batch, seq=seq, heads=gla_heads,
                     dk_h=gla_dk_h, dv_h=gla_dv_h)
            h = _outproj(h, o, gla_w_out[a].astype(BF16))
        else:
            b = layer - n_gla
            if layer == n_gla:
                cos_t, sin_t = _rope_tables(pos, rope_dim)
                w_lat = mla_w_dkv[:, :kv_lora]
                w_rope = mla_w_dkv[:, kv_lora:]
                w_dkv = jnp.concatenate(
                    [w_lat, _pad_cols(w_rope, LANES), _pad_cols(_swap_halves(w_rope), LANES)], axis=1)
                k_cat, v_all = _mla_kv(h, row(kv_norm_in), w_dkv.astype(BF16), row(mla_kv_norm),
                                       mla_w_uk.astype(BF16), mla_w_uv.astype(BF16), cos_t, sin_t,
                                       heads=mla_heads)
            w_uq = mla_w_uq[b].reshape(-1, mla_heads, nope_dim + rope_dim)
            q_lora = w_uq.shape[0]
            w_n = w_uq[:, :, :nope_dim].reshape(q_lora, mla_heads * nope_dim)
            w_r = w_uq[:, :, nope_dim:]
            w_rp = _pad_heads(w_r.reshape(q_lora, -1), mla_heads)
            w_sp = _pad_heads(_swap_halves(w_r).reshape(q_lora, -1), mla_heads)
            q_cat = _mla_q(h, row(norm_mix[layer]), mla_w_dq[b].astype(BF16), row(mla_q_norm[b]),
                           w_n.astype(BF16), w_rp.astype(BF16), w_sp.astype(BF16), cos_t, sin_t,
                           heads=mla_heads, scale=attn_scale)
            o = _attention(q_cat, k_cat, v_all, batch=batch, seq=seq, heads=mla_heads)
            h = _outproj(h, o, mla_w_o[b].astype(BF16))
        h = _mlp(h, row(norm_mlp[layer]), mlp_w1[layer].astype(BF16), mlp_w2[layer].astype(BF16),
                 row(final_norm), final_norm=(layer == depth - 1))
    return h.reshape(batch, seq, d)
```

```python
import functools

import jax
import jax.numpy as jnp
from jax import lax
from jax.experimental import pallas as pl
from jax.experimental.pallas import tpu as pltpu

BF16 = jnp.bfloat16
F32 = jnp.float32

RMS_EPS = 1e-6
ROPE_THETA = 10000.0
GLA_TAU = 16.0
GLA_CHUNK = 64
GLA_SUB = 8
LANES = 128
NEG_BIG = -1e30
LOG2_E = 1.4426950408889634
ATTN_ROW_SPLIT = 2
ATTN_DIAG_ROW_SPLIT = 4
INPROJ_SPLIT = 4

VMEM_LIMIT_BYTES = 56 * 1024 * 1024

TILES = dict(
    inproj_m=1024, inproj_n=1024, cast_n=1024,
    gla_tokens=512,
    outproj_m=512,
    mlp_m=1024, mlp_f=512,
    kv_m=512,
    q_m=512,
    rope_m=2048,
    attn_q=1024, attn_k=1024, attn_heads=4,
)


def _cparams(sem):
    return pltpu.CompilerParams(dimension_semantics=sem, vmem_limit_bytes=VMEM_LIMIT_BYTES)


def _tile(n, t):
    t = min(n, t)
    assert n % t == 0, (n, t)
    return t


def _rms(x, g):
    return x * lax.rsqrt(jnp.mean(x * x, axis=-1, keepdims=True) + RMS_EPS) * g


def _dot(a, b):
    return jnp.dot(a, b, preferred_element_type=F32)


def _dot_nt(a, b):
    return lax.dot_general(a, b, (((1,), (1,)), ((), ())), preferred_element_type=F32)


def _dot_tn(a, b):
    return lax.dot_general(a, b, (((0,), (0,)), ((), ())), preferred_element_type=F32)


def _gla_inproj_kernel(x_ref, g_ref, w_ref, wg_ref, wup_ref, bg_ref, out_ref, la_ref, xn_ref, *, rank):
    @pl.when(pl.program_id(1) == 0)
    def _():
        xn = _rms(x_ref[...], g_ref[...]).astype(BF16)
        xn_ref[...] = xn
        lane = lax.broadcasted_iota(jnp.int32, wg_ref.shape, 1)
        w_gate = jnp.where(lane < rank, wg_ref[...], 0.0).astype(BF16)
        g_low = _dot(xn, w_gate).astype(BF16)
        tm, tn = out_ref.shape
        rb, cb = tm // INPROJ_SPLIT, tn // INPROJ_SPLIT
        for c in range(INPROJ_SPLIT):
            out_ref[:, c * cb:(c + 1) * cb] = _dot(xn, w_ref[:, c * cb:(c + 1) * cb]).astype(out_ref.dtype)
            z = _dot(g_low[c * rb:(c + 1) * rb, :], wup_ref[...]) + bg_ref[...]
            log_sig = jnp.minimum(z, 0.0) - jnp.log(1.0 + jnp.exp(-jnp.abs(z)))
            la_ref[c * rb:(c + 1) * rb, :] = log_sig * (LOG2_E / GLA_TAU)

    @pl.when(pl.program_id(1) != 0)
    def _():
        out_ref[...] = _dot(xn_ref[...], w_ref[...]).astype(out_ref.dtype)


def _gla_inproj(h, g, w, w_full, wup, bg):
    t, d = h.shape
    n = w.shape[1]
    dk = wup.shape[1]
    rank = w_full.shape[1] - n
    assert n % LANES == 0 and rank <= LANES
    tm = _tile(t, TILES["inproj_m"])
    tn = _tile(n, TILES["inproj_n"])
    return pl.pallas_call(
        functools.partial(_gla_inproj_kernel, rank=rank),
        grid=(t // tm, n // tn),
        in_specs=[
            pl.BlockSpec((tm, d), lambda i, j: (i, 0)),
            pl.BlockSpec((1, d), lambda i, j: (0, 0)),
            pl.BlockSpec((d, tn), lambda i, j: (0, j)),
            pl.BlockSpec((d, LANES), lambda i, j: (0, n // LANES)),
            pl.BlockSpec(wup.shape, lambda i, j: (0, 0)),
            pl.BlockSpec((1, dk), lambda i, j: (0, 0)),
        ],
        out_specs=[
            pl.BlockSpec((tm, tn), lambda i, j: (i, j)),
            pl.BlockSpec((tm, dk), lambda i, j: (i, 0)),
        ],
        out_shape=[
            jax.ShapeDtypeStruct((t, n), BF16),
            jax.ShapeDtypeStruct((t, dk), F32),
        ],
        scratch_shapes=[pltpu.VMEM((tm, d), BF16)],
        compiler_params=_cparams(("parallel", "arbitrary")),
        name="gla_inproj",
    )(h, g, w, w_full, wup, bg)


def _cast_cols_kernel(w_ref, out_ref):
    out_ref[...] = w_ref[...].astype(out_ref.dtype)


def _cast_cols(w, n):
    d = w.shape[0]
    tn = _tile(n, TILES["cast_n"])
    return pl.pallas_call(
        _cast_cols_kernel,
        grid=(n // tn,),
        in_specs=[pl.BlockSpec((d, tn), lambda j: (0, j))],
        out_specs=pl.BlockSpec((d, tn), lambda j: (0, j)),
        out_shape=jax.ShapeDtypeStruct((d, n), BF16),
        compiler_params=_cparams(("parallel",)),
        name="cast_cols",
    )(w)


def _gla_kernel(x_ref, la_ref, gn_ref, out_ref, st_ref, *, n_chunks, heads, dk_h, dv_h, scale):
    c_len, sub = GLA_CHUNK, GLA_SUB
    k_off = heads * dk_h
    v_off = 2 * heads * dk_h
    r_off = v_off + heads * dv_h

    @pl.when(pl.program_id(1) == 0)
    def _():
        st_ref[...] = jnp.zeros_like(st_ref)

    row = lax.broadcasted_iota(jnp.int32, (c_len, c_len), 0)
    col = lax.broadcasted_iota(jnp.int32, (c_len, c_len), 1)
    tril = (col <= row).astype(F32)
    causal = col <= row
    blk = 2 * sub
    n_blk = c_len // blk
    far = col < (row // blk) * blk
    near = jnp.logical_and(jnp.logical_and(row % blk >= sub, col // blk == row // blk), col % blk < sub)
    srow = lax.broadcasted_iota(jnp.int32, (sub * sub, c_len), 0)
    scol = lax.broadcasted_iota(jnp.int32, (sub * sub, c_len), 1)
    diag_base = scol - srow // sub
    gn = gn_ref[...]

    def load(h, rows):
        q = x_ref[rows, h * dk_h:(h + 1) * dk_h].astype(F32) * scale
        k = x_ref[rows, k_off + h * dk_h:k_off + (h + 1) * dk_h]
        v = x_ref[rows, v_off + h * dv_h:v_off + (h + 1) * dv_h]
        return q, k, k.astype(F32), v

    def inter(h, b, q, k_f, v):
        b_last = b[c_len - 1:c_len, :]
        st = st_ref[h]
        o = _dot_nt((q * jnp.exp2(b)).astype(BF16), st.astype(BF16))
        k_dec = (k_f * jnp.exp2(b_last - b)).astype(BF16)
        st_ref[h] = st * jnp.exp2(b_last) + _dot_tn(v, k_dec)
        return o

    def diag_scores(b, q, k):
        diag = []
        for s in range(c_len // sub):
            lo = s * sub
            b_s = b[lo:lo + sub, :]
            q_s = q[lo:lo + sub, :]
            stacked = jnp.concatenate(
                [q_s * jnp.exp2(b_s - b[lo + jj:lo + jj + 1, :]) for jj in range(sub)], axis=0)
            full = _dot_nt(stacked.astype(BF16), k)
            full = jnp.where(diag_base == lo, full, 0.0)
            diag.append(jnp.sum(full.reshape(sub, sub, c_len), axis=0))
        return jnp.concatenate(diag, axis=0)

    def near_scores(b, q, k_f):
        b_mid = jnp.concatenate(
            [jnp.broadcast_to(b[t * blk + sub:t * blk + sub + 1, :], (blk, dk_h)) for t in range(n_blk)], axis=0)
        return _dot_nt((q * jnp.exp2(b - b_mid)).astype(BF16), (k_f * jnp.exp2(b_mid - b)).astype(BF16))

    def far_scores(b, q, k_f):
        far_s = [jnp.zeros((blk, c_len), F32)]
        for t in range(1, n_blk):
            lo = t * blk
            b_ref = b[lo:lo + 1, :]
            q_t = (q[lo:lo + blk, :] * jnp.exp2(b[lo:lo + blk, :] - b_ref)).astype(BF16)
            k_t = jnp.concatenate([k_f[:lo, :] * jnp.exp2(b_ref - b[:lo, :]),
                                   jnp.zeros((c_len - lo, dk_h), F32)], axis=0).astype(BF16)
            far_s.append(_dot_nt(q_t, k_t))
        return jnp.concatenate(far_s, axis=0)

    def emit(h, rows, o, diag, near_s, far_s, v):
        scores = jnp.where(far, far_s, jnp.where(near, near_s, diag))
        scores = jnp.where(causal, scores, 0.0)
        o = _rms(o + _dot(scores.astype(BF16), v), gn)
        r = x_ref[rows, r_off + h * dv_h:r_off + (h + 1) * dv_h].astype(F32)
        out_ref[rows, h * dv_h:(h + 1) * dv_h] = (o * (r * jax.nn.sigmoid(r))).astype(out_ref.dtype)

    def cumsum(c):
        rows = pl.ds(pl.multiple_of(c * c_len, c_len), c_len)
        return jnp.dot(tril, la_ref[rows, :], precision=lax.Precision.HIGHEST,
                       preferred_element_type=F32)

    def chunk(c, b_all):
        b_next = cumsum(jnp.minimum(c + 1, n_chunks - 1))
        rows = pl.ds(pl.multiple_of(c * c_len, c_len), c_len)
        hs = range(heads)
        bs = [b_all[:, h * dk_h:(h + 1) * dk_h] for h in hs]
        qkv = [load(h, rows) for h in hs]
        o = [inter(h, bs[h], qkv[h][0], qkv[h][2], qkv[h][3]) for h in hs]
        prev = None
        for h in hs:
            dg = diag_scores(bs[h], qkv[h][0], qkv[h][1])
            nr = near_scores(bs[h], qkv[h][0], qkv[h][2])
            fr = far_scores(bs[h], qkv[h][0], qkv[h][2])
            if prev is not None:
                emit(*prev)
            prev = (h, rows, o[h], dg, nr, fr, qkv[h][3])
        emit(*prev)
        return b_next

    lax.fori_loop(0, n_chunks, chunk, cumsum(0))


def _gla(qkvr, log_a, gnorm, *, batch, seq, heads, dk_h, dv_h):
    t, n = qkvr.shape
    tt = _tile(seq, TILES["gla_tokens"])
    ns = seq // tt
    kern = functools.partial(_gla_kernel, n_chunks=tt // GLA_CHUNK, heads=heads, dk_h=dk_h, dv_h=dv_h,
                             scale=float(dk_h) ** -0.5)
    return pl.pallas_call(
        kern,
        grid=(batch, ns),
        in_specs=[
            pl.BlockSpec((tt, n), lambda b, i: (b * ns + i, 0)),
            pl.BlockSpec((tt, heads * dk_h), lambda b, i: (b * ns + i, 0)),
            pl.BlockSpec((1, dv_h), lambda b, i: (0, 0)),
        ],
        out_specs=pl.BlockSpec((tt, heads * dv_h), lambda b, i: (b * ns + i, 0)),
        out_shape=jax.ShapeDtypeStruct((t, heads * dv_h), BF16),
        scratch_shapes=[pltpu.VMEM((heads, dv_h, dk_h), F32)],
        compiler_params=_cparams(("parallel", "arbitrary")),
        name="gla_chunk",
    )(qkvr, log_a, gnorm)


def _outproj_kernel(h_ref, a_ref, w_ref, out_ref):
    out_ref[...] = h_ref[...] + _dot(a_ref[...], w_ref[...])


def _outproj(h, a, w):
    t, d = h.shape
    k = a.shape[1]
    tm = _tile(t, TILES["outproj_m"])
    return pl.pallas_call(
        _outproj_kernel,
        grid=(t // tm,),
        in_specs=[
            pl.BlockSpec((tm, d), lambda i: (i, 0)),
            pl.BlockSpec((tm, k), lambda i: (i, 0)),
            pl.BlockSpec((k, d), lambda i: (0, 0)),
        ],
        out_specs=pl.BlockSpec((tm, d), lambda i: (i, 0)),
        out_shape=jax.ShapeDtypeStruct((t, d), F32),
        compiler_params=_cparams(("parallel",)),
        name="outproj",
    )(h, a, w)


def _mlp_kernel(x_ref, g_ref, w1_ref, w2_ref, gf_ref, out_ref, xn_ref, *, final_norm):
    j = pl.program_id(1)

    @pl.when(j == 0)
    def _():
        x = x_ref[...]
        xn_ref[...] = _rms(x, g_ref[...]).astype(BF16)
        out_ref[...] = x

    hid = _dot(xn_ref[...], w1_ref[...])
    hid = jnp.square(jnp.maximum(hid, 0.0)).astype(BF16)
    out_ref[...] += _dot(hid, w2_ref[...])

    if final_norm:
        @pl.when(j == pl.num_programs(1) - 1)
        def _():
            out_ref[...] = _rms(out_ref[...], gf_ref[...])


def _mlp(h, g, w1, w2, gf, *, final_norm):
    t, d = h.shape
    f = w1.shape[1]
    tm = _tile(t, TILES["mlp_m"])
    tf = _tile(f, TILES["mlp_f"])
    return pl.pallas_call(
        functools.partial(_mlp_kernel, final_norm=final_norm),
        grid=(t // tm, f // tf),
        in_specs=[
            pl.BlockSpec((tm, d), lambda i, j: (i, 0)),
            pl.BlockSpec((1, d), lambda i, j: (0, 0)),
            pl.BlockSpec((d, tf), lambda i, j: (0, j)),
            pl.BlockSpec((tf, d), lambda i, j: (j, 0)),
            pl.BlockSpec((1, d), lambda i, j: (0, 0)),
        ],
        out_specs=pl.BlockSpec((tm, d), lambda i, j: (i, 0)),
        out_shape=jax.ShapeDtypeStruct((t, d), F32),
        scratch_shapes=[pltpu.VMEM((tm, d), BF16)],
        compiler_params=_cparams(("parallel", "arbitrary")),
        name="mlp",
    )(h, g, w1, w2, gf)


def _rope_table_kernel(pos_ref, freq_ref, sign_ref, keep_ref, cos_ref, sin_ref):
    ang = pos_ref[...].astype(F32) * freq_ref[...]
    cos_ref[...] = jnp.cos(ang) * keep_ref[...]
    sin_ref[...] = jnp.sin(ang) * sign_ref[...]


def _rope_tables(pos, rope_dim):
    t = pos.shape[0]
    half = rope_dim // 2
    freqs = ROPE_THETA ** (-jnp.arange(0, rope_dim, 2, dtype=F32) / rope_dim)
    pad = jnp.zeros((LANES - rope_dim,), F32)
    freq = jnp.concatenate([freqs, freqs, pad]).reshape(1, LANES)
    sign = jnp.concatenate([-jnp.ones((half,), F32), jnp.ones((half,), F32), pad]).reshape(1, LANES)
    keep = jnp.concatenate([jnp.ones((rope_dim,), F32), pad]).reshape(1, LANES)
    tm = _tile(t, TILES["rope_m"])
    row = pl.BlockSpec((1, LANES), lambda i: (0, 0))
    tab = pl.BlockSpec((tm, LANES), lambda i: (i, 0))
    return pl.pallas_call(
        _rope_table_kernel,
        grid=(t // tm,),
        in_specs=[pl.BlockSpec((tm, 1), lambda i: (i, 0)), row, row, row],
        out_specs=[tab, tab],
        out_shape=[jax.ShapeDtypeStruct((t, LANES), F32)] * 2,
        compiler_params=_cparams(("parallel",)),
        name="rope_tables",
    )(pos, freq, sign, keep)


def _mla_kv_kernel(x_ref, g_ref, wd_ref, gkv_ref, wuk_ref, wuv_ref, cos_ref, sin_ref,
                   k_ref, v_ref, *, heads, lora):
    xn = _rms(x_ref[...], g_ref[...]).astype(BF16)
    ckv = _dot(xn, wd_ref[...])
    c = _rms(ckv[:, :lora], gkv_ref[...]).astype(BF16)
    k_rope = ckv[:, lora:lora + LANES] * cos_ref[...] + ckv[:, lora + LANES:] * sin_ref[...]
    k_rope = k_rope.astype(BF16)
    k_nope = _dot(c, wuk_ref[...]).astype(BF16)
    v_ref[...] = _dot(c, wuv_ref[...]).astype(BF16)
    for h in range(heads):
        k_ref[:, 2 * h * LANES:(2 * h + 1) * LANES] = k_nope[:, h * LANES:(h + 1) * LANES]
        k_ref[:, (2 * h + 1) * LANES:(2 * h + 2) * LANES] = k_rope


def _mla_kv(h, g, wd, gkv, wuk, wuv, cos_t, sin_t, *, heads):
    t, d = h.shape
    lora = gkv.shape[1]
    tm = _tile(t, TILES["kv_m"])
    full = lambda a: pl.BlockSpec(a.shape, lambda i: (0, 0))
    return pl.pallas_call(
        functools.partial(_mla_kv_kernel, heads=heads, lora=lora),
        grid=(t // tm,),
        in_specs=[
            pl.BlockSpec((tm, d), lambda i: (i, 0)),
            full(g), full(wd), full(gkv), full(wuk), full(wuv),
            pl.BlockSpec((tm, LANES), lambda i: (i, 0)),
            pl.BlockSpec((tm, LANES), lambda i: (i, 0)),
        ],
        out_specs=[
            pl.BlockSpec((tm, 2 * heads * LANES), lambda i: (i, 0)),
            pl.BlockSpec((tm, heads * LANES), lambda i: (i, 0)),
        ],
        out_shape=[
            jax.ShapeDtypeStruct((t, 2 * heads * LANES), BF16),
            jax.ShapeDtypeStruct((t, heads * LANES), BF16),
        ],
        compiler_params=_cparams(("parallel",)),
        name="mla_kv",
    )(h, g, wd, gkv, wuk, wuv, cos_t, sin_t)


def _mla_q_kernel(x_ref, g_ref, wd_ref, gq_ref, wn_ref, wr_ref, ws_ref, cos_ref, sin_ref,
                  q_ref, *, heads, scale):
    xn = _rms(x_ref[...], g_ref[...]).astype(BF16)
    cq = _rms(_dot(xn, wd_ref[...]), gq_ref[...]).astype(BF16)
    q_nope = _dot(cq, wn_ref[...])
    q_rope = _dot(cq, wr_ref[...])
    q_swap = _dot(cq, ws_ref[...])
    cos_t = cos_ref[...]
    sin_t = sin_ref[...]
    for h in range(heads):
        sl = slice(h * LANES, (h + 1) * LANES)
        q_ref[:, 2 * h * LANES:(2 * h + 1) * LANES] = (q_nope[:, sl] * scale).astype(BF16)
        roped = q_rope[:, sl] * cos_t + q_swap[:, sl] * sin_t
        q_ref[:, (2 * h + 1) * LANES:(2 * h + 2) * LANES] = (roped * scale).astype(BF16)


def _mla_q(h, g, wd, gq, wn, wr, ws, cos_t, sin_t, *, heads, scale):
    t, d = h.shape
    tm = _tile(t, TILES["q_m"])
    full = lambda a: pl.BlockSpec(a.shape, lambda i: (0, 0))
    return pl.pallas_call(
        functools.partial(_mla_q_kernel, heads=heads, scale=scale),
        grid=(t // tm,),
        in_specs=[
            pl.BlockSpec((tm, d), lambda i: (i, 0)),
            full(g), full(wd), full(gq), full(wn), full(wr), full(ws),
            pl.BlockSpec((tm, LANES), lambda i: (i, 0)),
            pl.BlockSpec((tm, LANES), lambda i: (i, 0)),
        ],
        out_specs=pl.BlockSpec((tm, 2 * heads * LANES), lambda i: (i, 0)),
        out_shape=jax.ShapeDtypeStruct((t, 2 * heads * LANES), BF16),
        compiler_params=_cparams(("parallel",)),
        name="mla_q",
    )(h, g, wd, gq, wn, wr, ws, cos_t, sin_t)


def _mla_kv_weights_kernel(w_ref, out_ref, *, lora, rope):
    half = rope // 2
    zeros = jnp.zeros((w_ref.shape[0], LANES - rope), F32)
    r1 = w_ref[:, lora:lora + half]
    r2 = w_ref[:, lora + half:lora + rope]
    out_ref[:, :lora] = w_ref[:, :lora].astype(BF16)
    out_ref[:, lora:lora + LANES] = jnp.concatenate([r1, r2, zeros], axis=1).astype(BF16)
    out_ref[:, lora + LANES:] = jnp.concatenate([r2, r1, zeros], axis=1).astype(BF16)


def _mla_kv_weights(w_dkv, *, lora, rope):
    assert lora % LANES == 0
    return pl.pallas_call(
        functools.partial(_mla_kv_weights_kernel, lora=lora, rope=rope),
        out_shape=jax.ShapeDtypeStruct((w_dkv.shape[0], lora + 2 * LANES), BF16),
        compiler_params=_cparams(()),
        name="mla_kv_weights",
    )(w_dkv)


def _mla_q_weights_kernel(w_ref, wn_ref, wr_ref, ws_ref, *, heads, nope, rope):
    half = rope // 2
    zeros = jnp.zeros((w_ref.shape[0], LANES - rope), F32)
    for h in range(heads):
        base = h * (nope + rope)
        wn_ref[:, h * LANES:(h + 1) * LANES] = w_ref[:, base:base + nope].astype(BF16)
        r1 = w_ref[:, base + nope:base + nope + half]
        r2 = w_ref[:, base + nope + half:base + nope + rope]
        wr_ref[:, h * LANES:(h + 1) * LANES] = jnp.concatenate([r1, r2, zeros], axis=1).astype(BF16)
        ws_ref[:, h * LANES:(h + 1) * LANES] = jnp.concatenate([r2, r1, zeros], axis=1).astype(BF16)


def _mla_q_weights(w_uq, *, heads, nope, rope):
    k = w_uq.shape[0]
    out = jax.ShapeDtypeStruct((k, heads * LANES), BF16)
    return pl.pallas_call(
        functools.partial(_mla_q_weights_kernel, heads=heads, nope=nope, rope=rope),
        out_shape=[out, out, out],
        compiler_params=_cparams(()),
        name="mla_q_weights",
    )(w_uq)


def _attn_kernel(qi_ref, kj_ref, q_ref, k_ref, v_ref, out_ref, m_ref, acc_ref, *, group, tq, tk):
    p = pl.program_id(2)
    qi = qi_ref[p]
    kj = kj_ref[p]
    q0 = qi * tq
    k0 = kj * tk

    @pl.when(kj == 0)
    def _():
        m_ref[...] = jnp.full_like(m_ref, NEG_BIG)
        acc_ref[...] = jnp.zeros_like(acc_ref)

    ones = jnp.ones((tk, LANES), BF16)

    def step(masked):
        if masked:
            row = q0 + lax.broadcasted_iota(jnp.int32, (tq, tk), 0)
            col = k0 + lax.broadcasted_iota(jnp.int32, (tq, tk), 1)
            keep = col <= row

        def scores(g):
            q = q_ref[:, 2 * g * LANES:(2 * g + 2) * LANES]
            k = k_ref[:, 2 * g * LANES:(2 * g + 2) * LANES]
            s = _dot_nt(q, k)
            if masked:
                s = jnp.where(keep, s, NEG_BIG)
            return s

        def update(g, s):
            v_ext = jnp.concatenate([v_ref[:, g * LANES:(g + 1) * LANES], ones], axis=1)
            n_rows = ATTN_DIAG_ROW_SPLIT if masked else ATTN_ROW_SPLIT
            rb = tq // n_rows
            for i in range(n_rows):
                rows = slice(i * rb, (i + 1) * rb)
                kw = min(tk, -(-(i + 1) * rb // LANES) * LANES) if (masked and tq == tk) else tk
                s_i = s[rows, :kw]
                m_prev = m_ref[g, rows, :]
                m_new = jnp.maximum(m_prev, jnp.max(s_i, axis=1, keepdims=True))
                alpha = jnp.exp2(m_prev - m_new)
                pr = jnp.exp2(s_i - jnp.concatenate([m_new] * (kw // LANES), axis=1))
                acc_ref[g, rows, :] = (jnp.concatenate([alpha, alpha], axis=1) * acc_ref[g, rows, :]
                                       + _dot(pr.astype(BF16), v_ext[:kw, :]))
                m_ref[g, rows, :] = m_new

        s_prev = scores(0)
        for g in range(1, group):
            s_next = scores(g)
            update(g - 1, s_prev)
            s_prev = s_next
        update(group - 1, s_prev)

    crosses = k0 + tk - 1 > q0

    @pl.when(crosses)
    def _():
        step(True)

    @pl.when(jnp.logical_not(crosses))
    def _():
        step(False)

    @pl.when(k0 + tk >= q0 + tq)
    def _():
        for g in range(group):
            acc = acc_ref[g]
            out_ref[:, g * LANES:(g + 1) * LANES] = (acc[:, :LANES] / acc[:, LANES:]).astype(out_ref.dtype)


def _attention(q, k, v, *, batch, seq, heads):
    t = q.shape[0]
    tq = _tile(seq, TILES["attn_q"])
    tk = _tile(tq, TILES["attn_k"])
    group = _tile(heads, TILES["attn_heads"])
    nq, nk = seq // tq, seq // tk
    pairs = [(i, j) for i in range(nq) for j in range(((i + 1) * tq - 1) // tk + 1)]
    qi_tab = jnp.asarray([a for a, _ in pairs], jnp.int32)
    kj_tab = jnp.asarray([b for _, b in pairs], jnp.int32)

    grid_spec = pltpu.PrefetchScalarGridSpec(
        num_scalar_prefetch=2,
        grid=(batch, heads // group, len(pairs)),
        in_specs=[
            pl.BlockSpec((tq, 2 * group * LANES), lambda b, h, p, qi, kj: (b * nq + qi[p], h)),
            pl.BlockSpec((tk, 2 * group * LANES), lambda b, h, p, qi, kj: (b * nk + kj[p], h)),
            pl.BlockSpec((tk, group * LANES), lambda b, h, p, qi, kj: (b * nk + kj[p], h)),
        ],
        out_specs=pl.BlockSpec((tq, group * LANES), lambda b, h, p, qi, kj: (b * nq + qi[p], h)),
        scratch_shapes=[pltpu.VMEM((group, tq, LANES), F32), pltpu.VMEM((group, tq, 2 * LANES), F32)],
    )
    return pl.pallas_call(
        functools.partial(_attn_kernel, group=group, tq=tq, tk=tk),
        grid_spec=grid_spec,
        out_shape=jax.ShapeDtypeStruct((t, heads * LANES), BF16),
        compiler_params=_cparams(("parallel", "parallel", "arbitrary")),
        name="mla_attention",
    )(qi_tab, kj_tab, q, k, v)


def kernel(x, positions, norm_mix, norm_mlp, gla_w_in, gla_w_gate_up, gla_b_gate, gla_norm, gla_w_out,
           kv_norm_in, mla_w_dkv, mla_kv_norm, mla_w_uk, mla_w_uv, mla_w_dq, mla_q_norm, mla_w_uq,
           mla_w_o, mlp_w1, mlp_w2, final_norm):
    batch, seq, d = x.shape
    t = batch * seq
    depth = norm_mix.shape[0]
    n_gla = gla_w_in.shape[0]

    gla_dk = gla_w_gate_up.shape[2]
    gla_dv = gla_w_out.shape[1]
    gla_dv_h = gla_norm.shape[1]
    gla_heads = gla_dv // gla_dv_h
    gla_dk_h = gla_dk // gla_heads
    n_main = 2 * gla_dk + 2 * gla_dv

    kv_lora = mla_kv_norm.shape[0]
    rope_dim = mla_w_dkv.shape[1] - kv_lora
    mla_heads = (mla_w_uq.shape[2] - mla_w_uk.shape[1]) // rope_dim
    nope_dim = mla_w_uk.shape[1] // mla_heads
    assert nope_dim == LANES and mla_w_uv.shape[1] == mla_heads * LANES and rope_dim <= LANES
    assert gla_dk_h % LANES == 0 and gla_dv_h % LANES == 0
    attn_scale = float(nope_dim + rope_dim) ** -0.5 * LOG2_E

    h = x.reshape(t, d)
    pos = positions.reshape(t, 1)
    row = lambda a: a.reshape(1, -1)

    cos_t = sin_t = k_cat = v_all = None
    for layer in range(depth):
        if layer < n_gla:
            a = layer
            w_in = gla_w_in[a]
            w_main = _cast_cols(w_in, n_main)
            w_up = jnp.pad(gla_w_gate_up[a], ((0, LANES - gla_w_gate_up.shape[1]), (0, 0))).astype(BF16)
            qkvr, log_a = _gla_inproj(h, row(norm_mix[layer]), w_main, w_in, w_up, row(gla_b_gate[a]))
            o = _gla(qkvr, log_a, row(gla_norm[a]), batch=batch, seq=seq, heads=gla_heads,
                     dk_h=gla_dk_h, dv_h=gla_dv_h)
            h = _outproj(h, o, gla_w_out[a].astype(BF16))
        else:
            b = layer - n_gla
            if layer == n_gla:
                cos_t, sin_t = _rope_tables(pos, rope_dim)
                w_dkv = _mla_kv_weights(mla_w_dkv, lora=kv_lora, rope=rope_dim)
                k_cat, v_all = _mla_kv(h, row(kv_norm_in), w_dkv, row(mla_kv_norm),
                                       mla_w_uk.astype(BF16), mla_w_uv.astype(BF16), cos_t, sin_t,
                                       heads=mla_heads)
            w_n, w_rp, w_sp = _mla_q_weights(mla_w_uq[b], heads=mla_heads, nope=nope_dim, rope=rope_dim)
            q_cat = _mla_q(h, row(norm_mix[layer]), mla_w_dq[b].astype(BF16), row(mla_q_norm[b]),
                           w_n, w_rp, w_sp, cos_t, sin_t,
                           heads=mla_heads, scale=attn_scale)
            o = _attention(q_cat, k_cat, v_all, batch=batch, seq=seq, heads=mla_heads)
            h = _outproj(h, o, mla_w_o[b].astype(BF16))
        h = _mlp(h, row(norm_mlp[layer]), mlp_w1[layer].astype(BF16), mlp_w2[layer].astype(BF16),
                 row(final_norm), final_norm=(layer == depth - 1))
    return h.reshape(batch, seq, d)
```

```python
import functools

import jax
import jax.numpy as jnp
from jax import lax
from jax.experimental import pallas as pl
from jax.experimental.pallas import tpu as pltpu

BF16 = jnp.bfloat16
F32 = jnp.float32

RMS_EPS = 1e-6
ROPE_THETA = 10000.0
GLA_TAU = 16.0
GLA_CHUNK = 64
GLA_SUB = 8
LANES = 128
NEG_BIG = -1e30
LOG2_E = 1.4426950408889634
ATTN_ROW_SPLIT = 2
ATTN_DIAG_ROW_SPLIT = 4
INPROJ_SPLIT = 4

VMEM_LIMIT_BYTES = 56 * 1024 * 1024

TILES = dict(
    inproj_m=1024, inproj_n=1024, cast_n=1024,
    gla_tokens=512,
    outproj_m=512,
    mlp_m=1024, mlp_f=512,
    kv_m=512,
    q_m=512,
    rope_m=2048,
    attn_q=1024, attn_k=1024, attn_heads=4,
)


def _cparams(sem):
    return pltpu.CompilerParams(dimension_semantics=sem, vmem_limit_bytes=VMEM_LIMIT_BYTES)


def _tile(n, t):
    t = min(n, t)
    assert n % t == 0, (n, t)
    return t


def _rms(x, g):
    return x * lax.rsqrt(jnp.mean(x * x, axis=-1, keepdims=True) + RMS_EPS) * g


def _dot(a, b):
    return jnp.dot(a, b, preferred_element_type=F32)


def _dot_nt(a, b):
    return lax.dot_general(a, b, (((1,), (1,)), ((), ())), preferred_element_type=F32)


def _dot_tn(a, b):
    return lax.dot_general(a, b, (((0,), (0,)), ((), ())), preferred_element_type=F32)


def _gla_inproj_kernel(x_ref, g_ref, w_ref, wg_ref, wup_ref, bg_ref, out_ref, la_ref, xn_ref, *, rank):
    @pl.when(pl.program_id(1) == 0)
    def _():
        xn = _rms(x_ref[...], g_ref[...]).astype(BF16)
        xn_ref[...] = xn
        lane = lax.broadcasted_iota(jnp.int32, wg_ref.shape, 1)
        w_gate = jnp.where(lane < rank, wg_ref[...], 0.0).astype(BF16)
        g_low = _dot(xn, w_gate).astype(BF16)
        tm, tn = out_ref.shape
        rb, cb = tm // INPROJ_SPLIT, tn // INPROJ_SPLIT
        for c in range(INPROJ_SPLIT):
            out_ref[:, c * cb:(c + 1) * cb] = _dot(xn, w_ref[:, c * cb:(c + 1) * cb]).astype(out_ref.dtype)
            z = _dot(g_low[c * rb:(c + 1) * rb, :], wup_ref[...]) + bg_ref[...]
            log_sig = jnp.minimum(z, 0.0) - jnp.log(1.0 + jnp.exp(-jnp.abs(z)))
            la_ref[c * rb:(c + 1) * rb, :] = log_sig * (LOG2_E / GLA_TAU)

    @pl.when(pl.program_id(1) != 0)
    def _():
        out_ref[...] = _dot(xn_ref[...], w_ref[...]).astype(out_ref.dtype)


def _gla_inproj(h, g, w, w_full, layer, wup, bg):
    t, d = h.shape
    n = w.shape[1]
    dk = wup.shape[1]
    rank = w_full.shape[2] - n
    assert n % LANES == 0 and rank <= LANES
    tm = _tile(t, TILES["inproj_m"])
    tn = _tile(n, TILES["inproj_n"])
    return pl.pallas_call(
        functools.partial(_gla_inproj_kernel, rank=rank),
        grid=(t // tm, n // tn),
        in_specs=[
            pl.BlockSpec((tm, d), lambda i, j: (i, 0)),
            pl.BlockSpec((1, d), lambda i, j: (0, 0)),
            pl.BlockSpec((d, tn), lambda i, j: (0, j)),
            pl.BlockSpec((None, d, LANES), lambda i, j: (layer, 0, n // LANES)),
            pl.BlockSpec(wup.shape, lambda i, j: (0, 0)),
            pl.BlockSpec((1, dk), lambda i, j: (0, 0)),
        ],
        out_specs=[
            pl.BlockSpec((tm, tn), lambda i, j: (i, j)),
            pl.BlockSpec((tm, dk), lambda i, j: (i, 0)),
        ],
        out_shape=[
            jax.ShapeDtypeStruct((t, n), BF16),
            jax.ShapeDtypeStruct((t, dk), F32),
        ],
        scratch_shapes=[pltpu.VMEM((tm, d), BF16)],
        compiler_params=_cparams(("parallel", "arbitrary")),
        name="gla_inproj",
    )(h, g, w, w_full, wup, bg)


def _cast_cols_kernel(w_ref, out_ref):
    out_ref[...] = w_ref[...].astype(out_ref.dtype)


def _cast_cols(w, layer, n):
    d = w.shape[1]
    tn = _tile(n, TILES["cast_n"])
    return pl.pallas_call(
        _cast_cols_kernel,
        grid=(n // tn,),
        in_specs=[pl.BlockSpec((None, d, tn), lambda j: (layer, 0, j))],
        out_specs=pl.BlockSpec((d, tn), lambda j: (0, j)),
        out_shape=jax.ShapeDtypeStruct((d, n), BF16),
        compiler_params=_cparams(("parallel",)),
        name="cast_cols",
    )(w)


def _gla_kernel(x_ref, la_ref, gn_ref, out_ref, st_ref, *, n_chunks, heads, dk_h, dv_h, scale):
    c_len, sub = GLA_CHUNK, GLA_SUB
    k_off = heads * dk_h
    v_off = 2 * heads * dk_h
    r_off = v_off + heads * dv_h

    @pl.when(pl.program_id(1) == 0)
    def _():
        st_ref[...] = jnp.zeros_like(st_ref)

    row = lax.broadcasted_iota(jnp.int32, (c_len, c_len), 0)
    col = lax.broadcasted_iota(jnp.int32, (c_len, c_len), 1)
    tril = (col <= row).astype(F32)
    causal = col <= row
    blk = 2 * sub
    n_blk = c_len // blk
    far = col < (row // blk) * blk
    near = jnp.logical_and(jnp.logical_and(row % blk >= sub, col // blk == row // blk), col % blk < sub)
    srow = lax.broadcasted_iota(jnp.int32, (sub * sub, c_len), 0)
    scol = lax.broadcasted_iota(jnp.int32, (sub * sub, c_len), 1)
    diag_base = scol - srow // sub
    gn = gn_ref[...]

    def load(h, rows):
        q = x_ref[rows, h * dk_h:(h + 1) * dk_h].astype(F32) * scale
        k = x_ref[rows, k_off + h * dk_h:k_off + (h + 1) * dk_h]
        v = x_ref[rows, v_off + h * dv_h:v_off + (h + 1) * dv_h]
        return q, k, k.astype(F32), v

    def inter(h, b, q, k_f, v):
        b_last = b[c_len - 1:c_len, :]
        st = st_ref[h]
        o = _dot_nt((q * jnp.exp2(b)).astype(BF16), st.astype(BF16))
        k_dec = (k_f * jnp.exp2(b_last - b)).astype(BF16)
        st_ref[h] = st * jnp.exp2(b_last) + _dot_tn(v, k_dec)
        return o

    def diag_scores(b, q, k):
        diag = []
        for s in range(c_len // sub):
            lo = s * sub
            b_s = b[lo:lo + sub, :]
            q_s = q[lo:lo + sub, :]
            stacked = jnp.concatenate(
                [q_s * jnp.exp2(b_s - b[lo + jj:lo + jj + 1, :]) for jj in range(sub)], axis=0)
            full = _dot_nt(stacked.astype(BF16), k)
            full = jnp.where(diag_base == lo, full, 0.0)
            diag.append(jnp.sum(full.reshape(sub, sub, c_len), axis=0))
        return jnp.concatenate(diag, axis=0)

    def near_scores(b, q, k_f):
        b_mid = jnp.concatenate(
            [jnp.broadcast_to(b[t * blk + sub:t * blk + sub + 1, :], (blk, dk_h)) for t in range(n_blk)], axis=0)
        return _dot_nt((q * jnp.exp2(b - b_mid)).astype(BF16), (k_f * jnp.exp2(b_mid - b)).astype(BF16))

    def far_scores(b, q, k_f):
        far_s = [jnp.zeros((blk, c_len), F32)]
        for t in range(1, n_blk):
            lo = t * blk
            b_ref = b[lo:lo + 1, :]
            q_t = (q[lo:lo + blk, :] * jnp.exp2(b[lo:lo + blk, :] - b_ref)).astype(BF16)
            k_t = jnp.concatenate([k_f[:lo, :] * jnp.exp2(b_ref - b[:lo, :]),
                                   jnp.zeros((c_len - lo, dk_h), F32)], axis=0).astype(BF16)
            far_s.append(_dot_nt(q_t, k_t))
        return jnp.concatenate(far_s, axis=0)

    def emit(h, rows, o, diag, near_s, far_s, v):
        scores = jnp.where(far, far_s, jnp.where(near, near_s, diag))
        scores = jnp.where(causal, scores, 0.0)
        o = _rms(o + _dot(scores.astype(BF16), v), gn)
        r = x_ref[rows, r_off + h * dv_h:r_off + (h + 1) * dv_h].astype(F32)
        out_ref[rows, h * dv_h:(h + 1) * dv_h] = (o * (r * jax.nn.sigmoid(r))).astype(out_ref.dtype)

    def cumsum(c):
        rows = pl.ds(pl.multiple_of(c * c_len, c_len), c_len)
        return jnp.dot(tril, la_ref[rows, :], precision=lax.Precision.HIGHEST,
                       preferred_element_type=F32)

    def chunk(c, b_all):
        b_next = cumsum(jnp.minimum(c + 1, n_chunks - 1))
        rows = pl.ds(pl.multiple_of(c * c_len, c_len), c_len)
        hs = range(heads)
        bs = [b_all[:, h * dk_h:(h + 1) * dk_h] for h in hs]
        qkv = [load(h, rows) for h in hs]
        o = [inter(h, bs[h], qkv[h][0], qkv[h][2], qkv[h][3]) for h in hs]
        prev = None
        for h in hs:
            dg = diag_scores(bs[h], qkv[h][0], qkv[h][1])
            nr = near_scores(bs[h], qkv[h][0], qkv[h][2])
            fr = far_scores(bs[h], qkv[h][0], qkv[h][2])
            if prev is not None:
                emit(*prev)
            prev = (h, rows, o[h], dg, nr, fr, qkv[h][3])
        emit(*prev)
        return b_next

    lax.fori_loop(0, n_chunks, chunk, cumsum(0))


def _gla(qkvr, log_a, gnorm, *, batch, seq, heads, dk_h, dv_h):
    t, n = qkvr.shape
    tt = _tile(seq, TILES["gla_tokens"])
    ns = seq // tt
    kern = functools.partial(_gla_kernel, n_chunks=tt // GLA_CHUNK, heads=heads, dk_h=dk_h, dv_h=dv_h,
                             scale=float(dk_h) ** -0.5)
    return pl.pallas_call(
        kern,
        grid=(batch, ns),
        in_specs=[
            pl.BlockSpec((tt, n), lambda b, i: (b * ns + i, 0)),
            pl.BlockSpec((tt, heads * dk_h), lambda b, i: (b * ns + i, 0)),
            pl.BlockSpec((1, dv_h), lambda b, i: (0, 0)),
        ],
        out_specs=pl.BlockSpec((tt, heads * dv_h), lambda b, i: (b * ns + i, 0)),
        out_shape=jax.ShapeDtypeStruct((t, heads * dv_h), BF16),
        scratch_shapes=[pltpu.VMEM((heads, dv_h, dk_h), F32)],
        compiler_params=_cparams(("parallel", "arbitrary")),
        name="gla_chunk",
    )(qkvr, log_a, gnorm)


def _outproj_kernel(h_ref, a_ref, w_ref, out_ref):
    out_ref[...] = h_ref[...] + _dot(a_ref[...], w_ref[...])


def _outproj(h, a, w):
    t, d = h.shape
    k = a.shape[1]
    tm = _tile(t, TILES["outproj_m"])
    return pl.pallas_call(
        _outproj_kernel,
        grid=(t // tm,),
        in_specs=[
            pl.BlockSpec((tm, d), lambda i: (i, 0)),
            pl.BlockSpec((tm, k), lambda i: (i, 0)),
            pl.BlockSpec((k, d), lambda i: (0, 0)),
        ],
        out_specs=pl.BlockSpec((tm, d), lambda i: (i, 0)),
        out_shape=jax.ShapeDtypeStruct((t, d), F32),
        compiler_params=_cparams(("parallel",)),
        name="outproj",
    )(h, a, w)


def _mlp_kernel(x_ref, g_ref, w1_ref, w2_ref, gf_ref, out_ref, xn_ref, *, final_norm):
    j = pl.program_id(1)

    @pl.when(j == 0)
    def _():
        x = x_ref[...]
        xn_ref[...] = _rms(x, g_ref[...]).astype(BF16)
        out_ref[...] = x

    hid = _dot(xn_ref[...], w1_ref[...])
    hid = jnp.square(jnp.maximum(hid, 0.0)).astype(BF16)
    out_ref[...] += _dot(hid, w2_ref[...])

    if final_norm:
        @pl.when(j == pl.num_programs(1) - 1)
        def _():
            out_ref[...] = _rms(out_ref[...], gf_ref[...])


def _mlp(h, g, w1, w2, gf, *, final_norm):
    t, d = h.shape
    f = w1.shape[1]
    tm = _tile(t, TILES["mlp_m"])
    tf = _tile(f, TILES["mlp_f"])
    return pl.pallas_call(
        functools.partial(_mlp_kernel, final_norm=final_norm),
        grid=(t // tm, f // tf),
        in_specs=[
            pl.BlockSpec((tm, d), lambda i, j: (i, 0)),
            pl.BlockSpec((1, d), lambda i, j: (0, 0)),
            pl.BlockSpec((d, tf), lambda i, j: (0, j)),
            pl.BlockSpec((tf, d), lambda i, j: (j, 0)),
            pl.BlockSpec((1, d), lambda i, j: (0, 0)),
        ],
        out_specs=pl.BlockSpec((tm, d), lambda i, j: (i, 0)),
        out_shape=jax.ShapeDtypeStruct((t, d), F32),
        scratch_shapes=[pltpu.VMEM((tm, d), BF16)],
        compiler_params=_cparams(("parallel", "arbitrary")),
        name="mlp",
    )(h, g, w1, w2, gf)


def _rope_table_kernel(pos_ref, freq_ref, sign_ref, keep_ref, cos_ref, sin_ref):
    ang = pos_ref[...].astype(F32) * freq_ref[...]
    cos_ref[...] = jnp.cos(ang) * keep_ref[...]
    sin_ref[...] = jnp.sin(ang) * sign_ref[...]


def _rope_tables(pos, rope_dim):
    t = pos.shape[0]
    half = rope_dim // 2
    freqs = ROPE_THETA ** (-jnp.arange(0, rope_dim, 2, dtype=F32) / rope_dim)
    pad = jnp.zeros((LANES - rope_dim,), F32)
    freq = jnp.concatenate([freqs, freqs, pad]).reshape(1, LANES)
    sign = jnp.concatenate([-jnp.ones((half,), F32), jnp.ones((half,), F32), pad]).reshape(1, LANES)
    keep = jnp.concatenate([jnp.ones((rope_dim,), F32), pad]).reshape(1, LANES)
    tm = _tile(t, TILES["rope_m"])
    row = pl.BlockSpec((1, LANES), lambda i: (0, 0))
    tab = pl.BlockSpec((tm, LANES), lambda i: (i, 0))
    return pl.pallas_call(
        _rope_table_kernel,
        grid=(t // tm,),
        in_specs=[pl.BlockSpec((tm, 1), lambda i: (i, 0)), row, row, row],
        out_specs=[tab, tab],
        out_shape=[jax.ShapeDtypeStruct((t, LANES), F32)] * 2,
        compiler_params=_cparams(("parallel",)),
        name="rope_tables",
    )(pos, freq, sign, keep)


def _mla_kv_kernel(x_ref, g_ref, wd_ref, gkv_ref, wuk_ref, wuv_ref, cos_ref, sin_ref,
                   k_ref, v_ref, *, heads, lora):
    xn = _rms(x_ref[...], g_ref[...]).astype(BF16)
    ckv = _dot(xn, wd_ref[...])
    c = _rms(ckv[:, :lora], gkv_ref[...]).astype(BF16)
    k_rope = ckv[:, lora:lora + LANES] * cos_ref[...] + ckv[:, lora + LANES:] * sin_ref[...]
    k_rope = k_rope.astype(BF16)
    k_nope = _dot(c, wuk_ref[...]).astype(BF16)
    v_ref[...] = _dot(c, wuv_ref[...]).astype(BF16)
    for h in range(heads):
        k_ref[:, 2 * h * LANES:(2 * h + 1) * LANES] = k_nope[:, h * LANES:(h + 1) * LANES]
        k_ref[:, (2 * h + 1) * LANES:(2 * h + 2) * LANES] = k_rope


def _mla_kv(h, g, wd, gkv, wuk, wuv, cos_t, sin_t, *, heads):
    t, d = h.shape
    lora = gkv.shape[1]
    tm = _tile(t, TILES["kv_m"])
    full = lambda a: pl.BlockSpec(a.shape, lambda i: (0, 0))
    return pl.pallas_call(
        functools.partial(_mla_kv_kernel, heads=heads, lora=lora),
        grid=(t // tm,),
        in_specs=[
            pl.BlockSpec((tm, d), lambda i: (i, 0)),
            full(g), full(wd), full(gkv), full(wuk), full(wuv),
            pl.BlockSpec((tm, LANES), lambda i: (i, 0)),
            pl.BlockSpec((tm, LANES), lambda i: (i, 0)),
        ],
        out_specs=[
            pl.BlockSpec((tm, 2 * heads * LANES), lambda i: (i, 0)),
            pl.BlockSpec((tm, heads * LANES), lambda i: (i, 0)),
        ],
        out_shape=[
            jax.ShapeDtypeStruct((t, 2 * heads * LANES), BF16),
            jax.ShapeDtypeStruct((t, heads * LANES), BF16),
        ],
        compiler_params=_cparams(("parallel",)),
        name="mla_kv",
    )(h, g, wd, gkv, wuk, wuv, cos_t, sin_t)


def _mla_q_kernel(x_ref, g_ref, wd_ref, gq_ref, wn_ref, wr_ref, ws_ref, cos_ref, sin_ref,
                  q_ref, *, heads, scale):
    xn = _rms(x_ref[...], g_ref[...]).astype(BF16)
    cq = _rms(_dot(xn, wd_ref[...]), gq_ref[...]).astype(BF16)
    q_nope = _dot(cq, wn_ref[...])
    q_rope = _dot(cq, wr_ref[...])
    q_swap = _dot(cq, ws_ref[...])
    cos_t = cos_ref[...]
    sin_t = sin_ref[...]
    for h in range(heads):
        sl = slice(h * LANES, (h + 1) * LANES)
        q_ref[:, 2 * h * LANES:(2 * h + 1) * LANES] = (q_nope[:, sl] * scale).astype(BF16)
        roped = q_rope[:, sl] * cos_t + q_swap[:, sl] * sin_t
        q_ref[:, (2 * h + 1) * LANES:(2 * h + 2) * LANES] = (roped * scale).astype(BF16)


def _mla_q(h, g, wd, gq, wn, wr, ws, cos_t, sin_t, *, heads, scale):
    t, d = h.shape
    tm = _tile(t, TILES["q_m"])
    full = lambda a: pl.BlockSpec(a.shape, lambda i: (0, 0))
    return pl.pallas_call(
        functools.partial(_mla_q_kernel, heads=heads, scale=scale),
        grid=(t // tm,),
        in_specs=[
            pl.BlockSpec((tm, d), lambda i: (i, 0)),
            full(g), full(wd), full(gq), full(wn), full(wr), full(ws),
            pl.BlockSpec((tm, LANES), lambda i: (i, 0)),
            pl.BlockSpec((tm, LANES), lambda i: (i, 0)),
        ],
        out_specs=pl.BlockSpec((tm, 2 * heads * LANES), lambda i: (i, 0)),
        out_shape=jax.ShapeDtypeStruct((t, 2 * heads * LANES), BF16),
        compiler_params=_cparams(("parallel",)),
        name="mla_q",
    )(h, g, wd, gq, wn, wr, ws, cos_t, sin_t)


def _mla_kv_weights_kernel(w_ref, out_ref, *, lora, rope):
    half = rope // 2
    zeros = jnp.zeros((w_ref.shape[0], LANES - rope), F32)
    r1 = w_ref[:, lora:lora + half]
    r2 = w_ref[:, lora + half:lora + rope]
    out_ref[:, :lora] = w_ref[:, :lora].astype(BF16)
    out_ref[:, lora:lora + LANES] = jnp.concatenate([r1, r2, zeros], axis=1).astype(BF16)
    out_ref[:, lora + LANES:] = jnp.concatenate([r2, r1, zeros], axis=1).astype(BF16)


def _mla_kv_weights(w_dkv, *, lora, rope):
    assert lora % LANES == 0
    return pl.pallas_call(
        functools.partial(_mla_kv_weights_kernel, lora=lora, rope=rope),
        out_shape=jax.ShapeDtypeStruct((w_dkv.shape[0], lora + 2 * LANES), BF16),
        compiler_params=_cparams(()),
        name="mla_kv_weights",
    )(w_dkv)


def _mla_q_weights_kernel(w_ref, wn_ref, wr_ref, ws_ref, *, heads, nope, rope):
    half = rope // 2
    zeros = jnp.zeros((w_ref.shape[0], LANES - rope), F32)
    for h in range(heads):
        base = h * (nope + rope)
        wn_ref[:, h * LANES:(h + 1) * LANES] = w_ref[:, base:base + nope].astype(BF16)
        r1 = w_ref[:, base + nope:base + nope + half]
        r2 = w_ref[:, base + nope + half:base + nope + rope]
        wr_ref[:, h * LANES:(h + 1) * LANES] = jnp.concatenate([r1, r2, zeros], axis=1).astype(BF16)
        ws_ref[:, h * LANES:(h + 1) * LANES] = jnp.concatenate([r2, r1, zeros], axis=1).astype(BF16)


def _mla_q_weights(w_uq, *, heads, nope, rope):
    k = w_uq.shape[0]
    out = jax.ShapeDtypeStruct((k, heads * LANES), BF16)
    return pl.pallas_call(
        functools.partial(_mla_q_weights_kernel, heads=heads, nope=nope, rope=rope),
        out_shape=[out, out, out],
        compiler_params=_cparams(()),
        name="mla_q_weights",
    )(w_uq)


def _attn_kernel(qi_ref, kj_ref, q_ref, k_ref, v_ref, out_ref, m_ref, acc_ref, *, group, tq, tk):
    p = pl.program_id(2)
    qi = qi_ref[p]
    kj = kj_ref[p]
    q0 = qi * tq
    k0 = kj * tk

    @pl.when(kj == 0)
    def _():
        m_ref[...] = jnp.full_like(m_ref, NEG_BIG)
        acc_ref[...] = jnp.zeros_like(acc_ref)

    ones = jnp.ones((tk, LANES), BF16)

    def step(masked):
        if masked:
            row = q0 + lax.broadcasted_iota(jnp.int32, (tq, tk), 0)
            col = k0 + lax.broadcasted_iota(jnp.int32, (tq, tk), 1)
            keep = col <= row

        def scores(g):
            q = q_ref[:, 2 * g * LANES:(2 * g + 2) * LANES]
            k = k_ref[:, 2 * g * LANES:(2 * g + 2) * LANES]
            s = _dot_nt(q, k)
            if masked:
                s = jnp.where(keep, s, NEG_BIG)
            return s

        def update(g, s):
            v_ext = jnp.concatenate([v_ref[:, g * LANES:(g + 1) * LANES], ones], axis=1)
            n_rows = ATTN_DIAG_ROW_SPLIT if masked else ATTN_ROW_SPLIT
            rb = tq // n_rows
            for i in range(n_rows):
                rows = slice(i * rb, (i + 1) * rb)
                kw = min(tk, -(-(i + 1) * rb // LANES) * LANES) if (masked and tq == tk) else tk
                s_i = s[rows, :kw]
                m_prev = m_ref[g, rows, :]
                m_new = jnp.maximum(m_prev, jnp.max(s_i, axis=1, keepdims=True))
                alpha = jnp.exp2(m_prev - m_new)
                pr = jnp.exp2(s_i - jnp.concatenate([m_new] * (kw // LANES), axis=1))
                acc_ref[g, rows, :] = (jnp.concatenate([alpha, alpha], axis=1) * acc_ref[g, rows, :]
                                       + _dot(pr.astype(BF16), v_ext[:kw, :]))
                m_ref[g, rows, :] = m_new

        s_prev = scores(0)
        for g in range(1, group):
            s_next = scores(g)
            update(g - 1, s_prev)
            s_prev = s_next
        update(group - 1, s_prev)

    crosses = k0 + tk - 1 > q0

    @pl.when(crosses)
    def _():
        step(True)

    @pl.when(jnp.logical_not(crosses))
    def _():
        step(False)

    @pl.when(k0 + tk >= q0 + tq)
    def _():
        for g in range(group):
            acc = acc_ref[g]
            out_ref[:, g * LANES:(g + 1) * LANES] = (acc[:, :LANES] / acc[:, LANES:]).astype(out_ref.dtype)


def _attention(q, k, v, *, batch, seq, heads):
    t = q.shape[0]
    tq = _tile(seq, TILES["attn_q"])
    tk = _tile(tq, TILES["attn_k"])
    group = _tile(heads, TILES["attn_heads"])
    nq, nk = seq // tq, seq // tk
    pairs = [(i, j) for i in range(nq) for j in range(((i + 1) * tq - 1) // tk + 1)]
    qi_tab = jnp.asarray([a for a, _ in pairs], jnp.int32)
    kj_tab = jnp.asarray([b for _, b in pairs], jnp.int32)

    grid_spec = pltpu.PrefetchScalarGridSpec(
        num_scalar_prefetch=2,
        grid=(batch, heads // group, len(pairs)),
        in_specs=[
            pl.BlockSpec((tq, 2 * group * LANES), lambda b, h, p, qi, kj: (b * nq + qi[p], h)),
            pl.BlockSpec((tk, 2 * group * LANES), lambda b, h, p, qi, kj: (b * nk + kj[p], h)),
            pl.BlockSpec((tk, group * LANES), lambda b, h, p, qi, kj: (b * nk + kj[p], h)),
        ],
        out_specs=pl.BlockSpec((tq, group * LANES), lambda b, h, p, qi, kj: (b * nq + qi[p], h)),
        scratch_shapes=[pltpu.VMEM((group, tq, LANES), F32), pltpu.VMEM((group, tq, 2 * LANES), F32)],
    )
    return pl.pallas_call(
        functools.partial(_attn_kernel, group=group, tq=tq, tk=tk),
        grid_spec=grid_spec,
        out_shape=jax.ShapeDtypeStruct((t, heads * LANES), BF16),
        compiler_params=_cparams(("parallel", "parallel", "arbitrary")),
        name="mla_attention",
    )(qi_tab, kj_tab, q, k, v)


def kernel(x, positions, norm_mix, norm_mlp, gla_w_in, gla_w_gate_up, gla_b_gate, gla_norm, gla_w_out,
           kv_norm_in, mla_w_dkv, mla_kv_norm, mla_w_uk, mla_w_uv, mla_w_dq, mla_q_norm, mla_w_uq,
           mla_w_o, mlp_w1, mlp_w2, final_norm):
    batch, seq, d = x.shape
    t = batch * seq
    depth = norm_mix.shape[0]
    n_gla = gla_w_in.shape[0]

    gla_dk = gla_w_gate_up.shape[2]
    gla_dv = gla_w_out.shape[1]
    gla_dv_h = gla_norm.shape[1]
    gla_heads = gla_dv // gla_dv_h
    gla_dk_h = gla_dk // gla_heads
    n_main = 2 * gla_dk + 2 * gla_dv

    kv_lora = mla_kv_norm.shape[0]
    rope_dim = mla_w_dkv.shape[1] - kv_lora
    mla_heads = (mla_w_uq.shape[2] - mla_w_uk.shape[1]) // rope_dim
    nope_dim = mla_w_uk.shape[1] // mla_heads
    assert nope_dim == LANES and mla_w_uv.shape[1] == mla_heads * LANES and rope_dim <= LANES
    assert gla_dk_h % LANES == 0 and gla_dv_h % LANES == 0
    attn_scale = float(nope_dim + rope_dim) ** -0.5 * LOG2_E

    h = x.reshape(t, d)
    pos = positions.reshape(t, 1)
    row = lambda a: a.reshape(1, -1)

    cos_t = sin_t = k_cat = v_all = None
    for layer in range(depth):
        if layer < n_gla:
            a = layer
            w_main = _cast_cols(gla_w_in, a, n_main)
            w_up = jnp.pad(gla_w_gate_up[a], ((0, LANES - gla_w_gate_up.shape[1]), (0, 0))).astype(BF16)
            qkvr, log_a = _gla_inproj(h, row(norm_mix[layer]), w_main, gla_w_in, a, w_up, row(gla_b_gate[a]))
            o = _gla(qkvr, log_a, row(gla_norm[a]), batch=batch, seq=seq, heads=gla_heads,
                     dk_h=gla_dk_h, dv_h=gla_dv_h)
            h = _outproj(h, o, gla_w_out[a].astype(BF16))
        else:
            b = layer - n_gla
            if layer == n_gla:
                cos_t, sin_t = _rope_tables(pos, rope_dim)
                w_dkv = _mla_kv_weights(mla_w_dkv, lora=kv_lora, rope=rope_dim)
                k_cat, v_all = _mla_kv(h, row(kv_norm_in), w_dkv, row(mla_kv_norm),
                                       mla_w_uk.astype(BF16), mla_w_uv.astype(BF16), cos_t, sin_t,
                                       heads=mla_heads)
            w_n, w_rp, w_sp = _mla_q_weights(mla_w_uq[b], heads=mla_heads, nope=nope_dim, rope=rope_dim)
            q_cat = _mla_q(h, row(norm_mix[layer]), mla_w_dq[b].astype(BF16), row(mla_q_norm[b]),
                           w_n, w_rp, w_sp, cos_t, sin_t,
                           heads=mla_heads, scale=attn_scale)
            o = _attention(q_cat, k_cat, v_all, batch=batch, seq=seq, heads=mla_heads)
            h = _outproj(h, o, mla_w_o[b].astype(BF16))
        h = _mlp(h, row(norm_mlp[layer]), mlp_w1[layer].astype(BF16), mlp_w2[layer].astype(BF16),
                 row(final_norm), final_norm=(layer == depth - 1))
    return h.reshape(batch, seq, d)
```

```python
import functools

import jax
import jax.numpy as jnp
from jax import lax
from jax.experimental import pallas as pl
from jax.experimental.pallas import tpu as pltpu

BF16 = jnp.bfloat16
F32 = jnp.float32

RMS_EPS = 1e-6
ROPE_THETA = 10000.0
GLA_TAU = 16.0
GLA_CHUNK = 64
GLA_SUB = 8
LANES = 128
NEG_BIG = -1e30
LOG2_E = 1.4426950408889634
ATTN_ROW_SPLIT = 2
ATTN_DIAG_ROW_SPLIT = 4
INPROJ_SPLIT = 4

VMEM_LIMIT_BYTES = 56 * 1024 * 1024

TILES = dict(
    inproj_m=1024, inproj_n=1024, cast_n=1024,
    gla_tokens=512,
    outproj_m=512,
    mlp_m=1024, mlp_f=1024,
    kv_m=512,
    q_m=512,
    rope_m=2048,
    attn_q=1024, attn_k=1024, attn_heads=4,
)


def _cparams(sem):
    return pltpu.CompilerParams(dimension_semantics=sem, vmem_limit_bytes=VMEM_LIMIT_BYTES)


def _tile(n, t):
    t = min(n, t)
    assert n % t == 0, (n, t)
    return t


def _rms(x, g):
    return x * lax.rsqrt(jnp.mean(x * x, axis=-1, keepdims=True) + RMS_EPS) * g


def _dot(a, b):
    return jnp.dot(a, b, preferred_element_type=F32)


def _dot_nt(a, b):
    return lax.dot_general(a, b, (((1,), (1,)), ((), ())), preferred_element_type=F32)


def _dot_tn(a, b):
    return lax.dot_general(a, b, (((0,), (0,)), ((), ())), preferred_element_type=F32)


def _gla_inproj_kernel(x_ref, g_ref, w_ref, wg_ref, wup_ref, bg_ref, out_ref, la_ref, xn_ref, *, rank):
    @pl.when(pl.program_id(1) == 0)
    def _():
        xn = _rms(x_ref[...], g_ref[...]).astype(BF16)
        xn_ref[...] = xn
        lane = lax.broadcasted_iota(jnp.int32, wg_ref.shape, 1)
        w_gate = jnp.where(lane < rank, wg_ref[...], 0.0).astype(BF16)
        g_low = _dot(xn, w_gate).astype(BF16)
        tm, tn = out_ref.shape
        rb, cb = tm // INPROJ_SPLIT, tn // INPROJ_SPLIT
        for c in range(INPROJ_SPLIT):
            out_ref[:, c * cb:(c + 1) * cb] = _dot(xn, w_ref[:, c * cb:(c + 1) * cb]).astype(out_ref.dtype)
            z = _dot(g_low[c * rb:(c + 1) * rb, :], wup_ref[...]) + bg_ref[...]
            log_sig = jnp.minimum(z, 0.0) - jnp.log(1.0 + jnp.exp(-jnp.abs(z)))
            la_ref[c * rb:(c + 1) * rb, :] = log_sig * (LOG2_E / GLA_TAU)

    @pl.when(pl.program_id(1) != 0)
    def _():
        out_ref[...] = _dot(xn_ref[...], w_ref[...]).astype(out_ref.dtype)


def _gla_inproj(h, g, w, w_full, layer, wup, bg):
    t, d = h.shape
    n = w.shape[1]
    dk = wup.shape[1]
    rank = w_full.shape[2] - n
    assert n % LANES == 0 and rank <= LANES
    tm = _tile(t, TILES["inproj_m"])
    tn = _tile(n, TILES["inproj_n"])
    return pl.pallas_call(
        functools.partial(_gla_inproj_kernel, rank=rank),
        grid=(t // tm, n // tn),
        in_specs=[
            pl.BlockSpec((tm, d), lambda i, j: (i, 0)),
            pl.BlockSpec((1, d), lambda i, j: (0, 0)),
            pl.BlockSpec((d, tn), lambda i, j: (0, j)),
            pl.BlockSpec((None, d, LANES), lambda i, j: (layer, 0, n // LANES)),
            pl.BlockSpec(wup.shape, lambda i, j: (0, 0)),
            pl.BlockSpec((1, dk), lambda i, j: (0, 0)),
        ],
        out_specs=[
            pl.BlockSpec((tm, tn), lambda i, j: (i, j)),
            pl.BlockSpec((tm, dk), lambda i, j: (i, 0)),
        ],
        out_shape=[
            jax.ShapeDtypeStruct((t, n), BF16),
            jax.ShapeDtypeStruct((t, dk), F32),
        ],
        scratch_shapes=[pltpu.VMEM((tm, d), BF16)],
        compiler_params=_cparams(("parallel", "arbitrary")),
        name="gla_inproj",
    )(h, g, w, w_full, wup, bg)


def _cast_cols_kernel(w_ref, out_ref):
    out_ref[...] = w_ref[...].astype(out_ref.dtype)


def _cast_cols(w, layer, n):
    d = w.shape[1]
    tn = _tile(n, TILES["cast_n"])
    return pl.pallas_call(
        _cast_cols_kernel,
        grid=(n // tn,),
        in_specs=[pl.BlockSpec((None, d, tn), lambda j: (layer, 0, j))],
        out_specs=pl.BlockSpec((d, tn), lambda j: (0, j)),
        out_shape=jax.ShapeDtypeStruct((d, n), BF16),
        compiler_params=_cparams(("parallel",)),
        name="cast_cols",
    )(w)


def _gla_kernel(x_ref, la_ref, gn_ref, out_ref, st_ref, *, n_chunks, heads, dk_h, dv_h, scale):
    c_len, sub = GLA_CHUNK, GLA_SUB
    k_off = heads * dk_h
    v_off = 2 * heads * dk_h
    r_off = v_off + heads * dv_h

    @pl.when(pl.program_id(1) == 0)
    def _():
        st_ref[...] = jnp.zeros_like(st_ref)

    row = lax.broadcasted_iota(jnp.int32, (c_len, c_len), 0)
    col = lax.broadcasted_iota(jnp.int32, (c_len, c_len), 1)
    tril = (col <= row).astype(F32)
    causal = col <= row
    blk = 2 * sub
    n_blk = c_len // blk
    far = col < (row // blk) * blk
    near = jnp.logical_and(jnp.logical_and(row % blk >= sub, col // blk == row // blk), col % blk < sub)
    srow = lax.broadcasted_iota(jnp.int32, (sub * sub, c_len), 0)
    scol = lax.broadcasted_iota(jnp.int32, (sub * sub, c_len), 1)
    diag_base = scol - srow // sub
    gn = gn_ref[...]

    def load(h, rows):
        q = x_ref[rows, h * dk_h:(h + 1) * dk_h].astype(F32) * scale
        k = x_ref[rows, k_off + h * dk_h:k_off + (h + 1) * dk_h]
        v = x_ref[rows, v_off + h * dv_h:v_off + (h + 1) * dv_h]
        return q, k, k.astype(F32), v

    def inter(h, b, q, k_f, v):
        b_last = b[c_len - 1:c_len, :]
        st = st_ref[h]
        o = _dot_nt((q * jnp.exp2(b)).astype(BF16), st.astype(BF16))
        k_dec = (k_f * jnp.exp2(b_last - b)).astype(BF16)
        st_ref[h] = st * jnp.exp2(b_last) + _dot_tn(v, k_dec)
        return o

    def diag_scores(b, q, k):
        diag = []
        for s in range(c_len // sub):
            lo = s * sub
            b_s = b[lo:lo + sub, :]
            q_s = q[lo:lo + sub, :]
            stacked = jnp.concatenate(
                [q_s * jnp.exp2(b_s - b[lo + jj:lo + jj + 1, :]) for jj in range(sub)], axis=0)
            full = _dot_nt(stacked.astype(BF16), k)
            full = jnp.where(diag_base == lo, full, 0.0)
            diag.append(jnp.sum(full.reshape(sub, sub, c_len), axis=0))
        return jnp.concatenate(diag, axis=0)

    def near_scores(b, q, k_f):
        b_mid = jnp.concatenate(
            [jnp.broadcast_to(b[t * blk + sub:t * blk + sub + 1, :], (blk, dk_h)) for t in range(n_blk)], axis=0)
        return _dot_nt((q * jnp.exp2(b - b_mid)).astype(BF16), (k_f * jnp.exp2(b_mid - b)).astype(BF16))

    def far_scores(b, q, k_f):
        far_s = [jnp.zeros((blk, c_len), F32)]
        for t in range(1, n_blk):
            lo = t * blk
            b_ref = b[lo:lo + 1, :]
            q_t = (q[lo:lo + blk, :] * jnp.exp2(b[lo:lo + blk, :] - b_ref)).astype(BF16)
            k_t = jnp.concatenate([k_f[:lo, :] * jnp.exp2(b_ref - b[:lo, :]),
                                   jnp.zeros((c_len - lo, dk_h), F32)], axis=0).astype(BF16)
            far_s.append(_dot_nt(q_t, k_t))
        return jnp.concatenate(far_s, axis=0)

    def emit(h, rows, o, diag, near_s, far_s, v):
        scores = jnp.where(far, far_s, jnp.where(near, near_s, diag))
        scores = jnp.where(causal, scores, 0.0)
        o = _rms(o + _dot(scores.astype(BF16), v), gn)
        r = x_ref[rows, r_off + h * dv_h:r_off + (h + 1) * dv_h].astype(F32)
        out_ref[rows, h * dv_h:(h + 1) * dv_h] = (o * (r * jax.nn.sigmoid(r))).astype(out_ref.dtype)

    def cumsum(c):
        rows = pl.ds(pl.multiple_of(c * c_len, c_len), c_len)
        return jnp.dot(tril, la_ref[rows, :], precision=lax.Precision.HIGHEST,
                       preferred_element_type=F32)

    def chunk(c, b_all):
        b_next = cumsum(jnp.minimum(c + 1, n_chunks - 1))
        rows = pl.ds(pl.multiple_of(c * c_len, c_len), c_len)
        hs = range(heads)
        bs = [b_all[:, h * dk_h:(h + 1) * dk_h] for h in hs]
        qkv = [load(h, rows) for h in hs]
        o = [inter(h, bs[h], qkv[h][0], qkv[h][2], qkv[h][3]) for h in hs]
        prev = None
        for h in hs:
            dg = diag_scores(bs[h], qkv[h][0], qkv[h][1])
            nr = near_scores(bs[h], qkv[h][0], qkv[h][2])
            fr = far_scores(bs[h], qkv[h][0], qkv[h][2])
            if prev is not None:
                emit(*prev)
            prev = (h, rows, o[h], dg, nr, fr, qkv[h][3])
        emit(*prev)
        return b_next

    lax.fori_loop(0, n_chunks, chunk, cumsum(0))


def _gla(qkvr, log_a, gnorm, *, batch, seq, heads, dk_h, dv_h):
    t, n = qkvr.shape
    tt = _tile(seq, TILES["gla_tokens"])
    ns = seq // tt
    kern = functools.partial(_gla_kernel, n_chunks=tt // GLA_CHUNK, heads=heads, dk_h=dk_h, dv_h=dv_h,
                             scale=float(dk_h) ** -0.5)
    return pl.pallas_call(
        kern,
        grid=(batch, ns),
        in_specs=[
            pl.BlockSpec((tt, n), lambda b, i: (b * ns + i, 0)),
            pl.BlockSpec((tt, heads * dk_h), lambda b, i: (b * ns + i, 0)),
            pl.BlockSpec((1, dv_h), lambda b, i: (0, 0)),
        ],
        out_specs=pl.BlockSpec((tt, heads * dv_h), lambda b, i: (b * ns + i, 0)),
        out_shape=jax.ShapeDtypeStruct((t, heads * dv_h), BF16),
        scratch_shapes=[pltpu.VMEM((heads, dv_h, dk_h), F32)],
        compiler_params=_cparams(("parallel", "arbitrary")),
        name="gla_chunk",
    )(qkvr, log_a, gnorm)


def _outproj_kernel(h_ref, a_ref, w_ref, out_ref):
    out_ref[...] = h_ref[...] + _dot(a_ref[...], w_ref[...])


def _outproj(h, a, w):
    t, d = h.shape
    k = a.shape[1]
    tm = _tile(t, TILES["outproj_m"])
    return pl.pallas_call(
        _outproj_kernel,
        grid=(t // tm,),
        in_specs=[
            pl.BlockSpec((tm, d), lambda i: (i, 0)),
            pl.BlockSpec((tm, k), lambda i: (i, 0)),
            pl.BlockSpec((k, d), lambda i: (0, 0)),
        ],
        out_specs=pl.BlockSpec((tm, d), lambda i: (i, 0)),
        out_shape=jax.ShapeDtypeStruct((t, d), F32),
        compiler_params=_cparams(("parallel",)),
        name="outproj",
    )(h, a, w)


def _mlp_kernel(x_ref, g_ref, w1_ref, w2_ref, gf_ref, out_ref, xn_ref, *, final_norm):
    j = pl.program_id(1)

    @pl.when(j == 0)
    def _():
        x = x_ref[...]
        xn_ref[...] = _rms(x, g_ref[...]).astype(BF16)
        out_ref[...] = x

    hid = _dot(xn_ref[...], w1_ref[...])
    hid = jnp.square(jnp.maximum(hid, 0.0)).astype(BF16)
    out_ref[...] += _dot(hid, w2_ref[...])

    if final_norm:
        @pl.when(j == pl.num_programs(1) - 1)
        def _():
            out_ref[...] = _rms(out_ref[...], gf_ref[...])


def _mlp(h, g, w1, w2, gf, *, final_norm):
    t, d = h.shape
    f = w1.shape[1]
    tm = _tile(t, TILES["mlp_m"])
    tf = _tile(f, TILES["mlp_f"])
    return pl.pallas_call(
        functools.partial(_mlp_kernel, final_norm=final_norm),
        grid=(t // tm, f // tf),
        in_specs=[
            pl.BlockSpec((tm, d), lambda i, j: (i, 0), pipeline_mode=pl.Buffered(1)),
            pl.BlockSpec((1, d), lambda i, j: (0, 0)),
            pl.BlockSpec((d, tf), lambda i, j: (0, j)),
            pl.BlockSpec((tf, d), lambda i, j: (j, 0)),
            pl.BlockSpec((1, d), lambda i, j: (0, 0)),
        ],
        out_specs=pl.BlockSpec((tm, d), lambda i, j: (i, 0)),
        out_shape=jax.ShapeDtypeStruct((t, d), F32),
        scratch_shapes=[pltpu.VMEM((tm, d), BF16)],
        compiler_params=_cparams(("parallel", "arbitrary")),
        name="mlp",
    )(h, g, w1, w2, gf)


def _rope_table_kernel(pos_ref, freq_ref, sign_ref, keep_ref, cos_ref, sin_ref):
    ang = pos_ref[...].astype(F32) * freq_ref[...]
    cos_ref[...] = jnp.cos(ang) * keep_ref[...]
    sin_ref[...] = jnp.sin(ang) * sign_ref[...]


def _rope_tables(pos, rope_dim):
    t = pos.shape[0]
    half = rope_dim // 2
    freqs = ROPE_THETA ** (-jnp.arange(0, rope_dim, 2, dtype=F32) / rope_dim)
    pad = jnp.zeros((LANES - rope_dim,), F32)
    freq = jnp.concatenate([freqs, freqs, pad]).reshape(1, LANES)
    sign = jnp.concatenate([-jnp.ones((half,), F32), jnp.ones((half,), F32), pad]).reshape(1, LANES)
    keep = jnp.concatenate([jnp.ones((rope_dim,), F32), pad]).reshape(1, LANES)
    tm = _tile(t, TILES["rope_m"])
    row = pl.BlockSpec((1, LANES), lambda i: (0, 0))
    tab = pl.BlockSpec((tm, LANES), lambda i: (i, 0))
    return pl.pallas_call(
        _rope_table_kernel,
        grid=(t // tm,),
        in_specs=[pl.BlockSpec((tm, 1), lambda i: (i, 0)), row, row, row],
        out_specs=[tab, tab],
        out_shape=[jax.ShapeDtypeStruct((t, LANES), F32)] * 2,
        compiler_params=_cparams(("parallel",)),
        name="rope_tables",
    )(pos, freq, sign, keep)


def _mla_kv_kernel(x_ref, g_ref, wd_ref, gkv_ref, wuk_ref, wuv_ref, cos_ref, sin_ref,
                   k_ref, v_ref, *, heads, lora):
    xn = _rms(x_ref[...], g_ref[...]).astype(BF16)
    ckv = _dot(xn, wd_ref[...])
    c = _rms(ckv[:, :lora], gkv_ref[...]).astype(BF16)
    k_rope = ckv[:, lora:lora + LANES] * cos_ref[...] + ckv[:, lora + LANES:] * sin_ref[...]
    k_rope = k_rope.astype(BF16)
    k_nope = _dot(c, wuk_ref[...]).astype(BF16)
    v_ref[...] = _dot(c, wuv_ref[...]).astype(BF16)
    for h in range(heads):
        k_ref[:, 2 * h * LANES:(2 * h + 1) * LANES] = k_nope[:, h * LANES:(h + 1) * LANES]
        k_ref[:, (2 * h + 1) * LANES:(2 * h + 2) * LANES] = k_rope


def _mla_kv(h, g, wd, gkv, wuk, wuv, cos_t, sin_t, *, heads):
    t, d = h.shape
    lora = gkv.shape[1]
    tm = _tile(t, TILES["kv_m"])
    full = lambda a: pl.BlockSpec(a.shape, lambda i: (0, 0))
    return pl.pallas_call(
        functools.partial(_mla_kv_kernel, heads=heads, lora=lora),
        grid=(t // tm,),
        in_specs=[
            pl.BlockSpec((tm, d), lambda i: (i, 0)),
            full(g), full(wd), full(gkv), full(wuk), full(wuv),
            pl.BlockSpec((tm, LANES), lambda i: (i, 0)),
            pl.BlockSpec((tm, LANES), lambda i: (i, 0)),
        ],
        out_specs=[
            pl.BlockSpec((tm, 2 * heads * LANES), lambda i: (i, 0)),
            pl.BlockSpec((tm, heads * LANES), lambda i: (i, 0)),
        ],
        out_shape=[
            jax.ShapeDtypeStruct((t, 2 * heads * LANES), BF16),
            jax.ShapeDtypeStruct((t, heads * LANES), BF16),
        ],
        compiler_params=_cparams(("parallel",)),
        name="mla_kv",
    )(h, g, wd, gkv, wuk, wuv, cos_t, sin_t)


def _mla_q_kernel(x_ref, g_ref, wd_ref, gq_ref, wn_ref, wr_ref, ws_ref, cos_ref, sin_ref,
                  q_ref, *, heads, scale):
    xn = _rms(x_ref[...], g_ref[...]).astype(BF16)
    cq = _rms(_dot(xn, wd_ref[...]), gq_ref[...]).astype(BF16)
    q_nope = _dot(cq, wn_ref[...])
    q_rope = _dot(cq, wr_ref[...])
    q_swap = _dot(cq, ws_ref[...])
    cos_t = cos_ref[...]
    sin_t = sin_ref[...]
    for h in range(heads):
        sl = slice(h * LANES, (h + 1) * LANES)
        q_ref[:, 2 * h * LANES:(2 * h + 1) * LANES] = (q_nope[:, sl] * scale).astype(BF16)
        roped = q_rope[:, sl] * cos_t + q_swap[:, sl] * sin_t
        q_ref[:, (2 * h + 1) * LANES:(2 * h + 2) * LANES] = (roped * scale).astype(BF16)


def _mla_q(h, g, wd, gq, wn, wr, ws, cos_t, sin_t, *, heads, scale):
    t, d = h.shape
    tm = _tile(t, TILES["q_m"])
    full = lambda a: pl.BlockSpec(a.shape, lambda i: (0, 0))
    return pl.pallas_call(
        functools.partial(_mla_q_kernel, heads=heads, scale=scale),
        grid=(t // tm,),
        in_specs=[
            pl.BlockSpec((tm, d), lambda i: (i, 0)),
            full(g), full(wd), full(gq), full(wn), full(wr), full(ws),
            pl.BlockSpec((tm, LANES), lambda i: (i, 0)),
            pl.BlockSpec((tm, LANES), lambda i: (i, 0)),
        ],
        out_specs=pl.BlockSpec((tm, 2 * heads * LANES), lambda i: (i, 0)),
        out_shape=jax.ShapeDtypeStruct((t, 2 * heads * LANES), BF16),
        compiler_params=_cparams(("parallel",)),
        name="mla_q",
    )(h, g, wd, gq, wn, wr, ws, cos_t, sin_t)


def _mla_kv_weights_kernel(w_ref, out_ref, *, lora, rope):
    half = rope // 2
    zeros = jnp.zeros((w_ref.shape[0], LANES - rope), F32)
    r1 = w_ref[:, lora:lora + half]
    r2 = w_ref[:, lora + half:lora + rope]
    out_ref[:, :lora] = w_ref[:, :lora].astype(BF16)
    out_ref[:, lora:lora + LANES] = jnp.concatenate([r1, r2, zeros], axis=1).astype(BF16)
    out_ref[:, lora + LANES:] = jnp.concatenate([r2, r1, zeros], axis=1).astype(BF16)


def _mla_kv_weights(w_dkv, *, lora, rope):
    assert lora % LANES == 0
    return pl.pallas_call(
        functools.partial(_mla_kv_weights_kernel, lora=lora, rope=rope),
        out_shape=jax.ShapeDtypeStruct((w_dkv.shape[0], lora + 2 * LANES), BF16),
        compiler_params=_cparams(()),
        name="mla_kv_weights",
    )(w_dkv)


def _mla_q_weights_kernel(w_ref, wn_ref, wr_ref, ws_ref, *, heads, nope, rope):
    half = rope // 2
    zeros = jnp.zeros((w_ref.shape[0], LANES - rope), F32)
    for h in range(heads):
        base = h * (nope + rope)
        wn_ref[:, h * LANES:(h + 1) * LANES] = w_ref[:, base:base + nope].astype(BF16)
        r1 = w_ref[:, base + nope:base + nope + half]
        r2 = w_ref[:, base + nope + half:base + nope + rope]
        wr_ref[:, h * LANES:(h + 1) * LANES] = jnp.concatenate([r1, r2, zeros], axis=1).astype(BF16)
        ws_ref[:, h * LANES:(h + 1) * LANES] = jnp.concatenate([r2, r1, zeros], axis=1).astype(BF16)


def _mla_q_weights(w_uq, *, heads, nope, rope):
    k = w_uq.shape[0]
    out = jax.ShapeDtypeStruct((k, heads * LANES), BF16)
    return pl.pallas_call(
        functools.partial(_mla_q_weights_kernel, heads=heads, nope=nope, rope=rope),
        out_shape=[out, out, out],
        compiler_params=_cparams(()),
        name="mla_q_weights",
    )(w_uq)


def _attn_kernel(qi_ref, kj_ref, q_ref, k_ref, v_ref, out_ref, m_ref, acc_ref, *, group, tq, tk):
    p = pl.program_id(2)
    qi = qi_ref[p]
    kj = kj_ref[p]
    q0 = qi * tq
    k0 = kj * tk

    @pl.when(kj == 0)
    def _():
        m_ref[...] = jnp.full_like(m_ref, NEG_BIG)
        acc_ref[...] = jnp.zeros_like(acc_ref)

    ones = jnp.ones((tk, LANES), BF16)

    def step(masked):
        if masked:
            row = q0 + lax.broadcasted_iota(jnp.int32, (tq, tk), 0)
            col = k0 + lax.broadcasted_iota(jnp.int32, (tq, tk), 1)
            keep = col <= row

        def scores(g):
            q = q_ref[:, 2 * g * LANES:(2 * g + 2) * LANES]
            k = k_ref[:, 2 * g * LANES:(2 * g + 2) * LANES]
            s = _dot_nt(q, k)
            if masked:
                s = jnp.where(keep, s, NEG_BIG)
            return s

        def update(g, s):
            v_ext = jnp.concatenate([v_ref[:, g * LANES:(g + 1) * LANES], ones], axis=1)
            n_rows = ATTN_DIAG_ROW_SPLIT if masked else ATTN_ROW_SPLIT
            rb = tq // n_rows
            for i in range(n_rows):
                rows = slice(i * rb, (i + 1) * rb)
                kw = min(tk, -(-(i + 1) * rb // LANES) * LANES) if (masked and tq == tk) else tk
                s_i = s[rows, :kw]
                m_prev = m_ref[g, rows, :]
                m_new = jnp.maximum(m_prev, jnp.max(s_i, axis=1, keepdims=True))
                alpha = jnp.exp2(m_prev - m_new)
                pr = jnp.exp2(s_i - jnp.concatenate([m_new] * (kw // LANES), axis=1))
                acc_ref[g, rows, :] = (jnp.concatenate([alpha, alpha], axis=1) * acc_ref[g, rows, :]
                                       + _dot(pr.astype(BF16), v_ext[:kw, :]))
                m_ref[g, rows, :] = m_new

        s_prev = scores(0)
        for g in range(1, group):
            s_next = scores(g)
            update(g - 1, s_prev)
            s_prev = s_next
        update(group - 1, s_prev)

    crosses = k0 + tk - 1 > q0

    @pl.when(crosses)
    def _():
        step(True)

    @pl.when(jnp.logical_not(crosses))
    def _():
        step(False)

    @pl.when(k0 + tk >= q0 + tq)
    def _():
        for g in range(group):
            acc = acc_ref[g]
            out_ref[:, g * LANES:(g + 1) * LANES] = (acc[:, :LANES] / acc[:, LANES:]).astype(out_ref.dtype)


def _attention(q, k, v, *, batch, seq, heads):
    t = q.shape[0]
    tq = _tile(seq, TILES["attn_q"])
    tk = _tile(tq, TILES["attn_k"])
    group = _tile(heads, TILES["attn_heads"])
    nq, nk = seq // tq, seq // tk
    pairs = [(i, j) for i in range(nq) for j in range(((i + 1) * tq - 1) // tk + 1)]
    qi_tab = jnp.asarray([a for a, _ in pairs], jnp.int32)
    kj_tab = jnp.asarray([b for _, b in pairs], jnp.int32)

    grid_spec = pltpu.PrefetchScalarGridSpec(
        num_scalar_prefetch=2,
        grid=(batch, heads // group, len(pairs)),
        in_specs=[
            pl.BlockSpec((tq, 2 * group * LANES), lambda b, h, p, qi, kj: (b * nq + qi[p], h)),
            pl.BlockSpec((tk, 2 * group * LANES), lambda b, h, p, qi, kj: (b * nk + kj[p], h)),
            pl.BlockSpec((tk, group * LANES), lambda b, h, p, qi, kj: (b * nk + kj[p], h)),
        ],
        out_specs=pl.BlockSpec((tq, group * LANES), lambda b, h, p, qi, kj: (b * nq + qi[p], h)),
        scratch_shapes=[pltpu.VMEM((group, tq, LANES), F32), pltpu.VMEM((group, tq, 2 * LANES), F32)],
    )
    return pl.pallas_call(
        functools.partial(_attn_kernel, group=group, tq=tq, tk=tk),
        grid_spec=grid_spec,
        out_shape=jax.ShapeDtypeStruct((t, heads * LANES), BF16),
        compiler_params=_cparams(("parallel", "parallel", "arbitrary")),
        name="mla_attention",
    )(qi_tab, kj_tab, q, k, v)


def kernel(x, positions, norm_mix, norm_mlp, gla_w_in, gla_w_gate_up, gla_b_gate, gla_norm, gla_w_out,
           kv_norm_in, mla_w_dkv, mla_kv_norm, mla_w_uk, mla_w_uv, mla_w_dq, mla_q_norm, mla_w_uq,
           mla_w_o, mlp_w1, mlp_w2, final_norm):
    batch, seq, d = x.shape
    t = batch * seq
    depth = norm_mix.shape[0]
    n_gla = gla_w_in.shape[0]

    gla_dk = gla_w_gate_up.shape[2]
    gla_dv = gla_w_out.shape[1]
    gla_dv_h = gla_norm.shape[1]
    gla_heads = gla_dv // gla_dv_h
    gla_dk_h = gla_dk // gla_heads
    n_main = 2 * gla_dk + 2 * gla_dv

    kv_lora = mla_kv_norm.shape[0]
    rope_dim = mla_w_dkv.shape[1] - kv_lora
    mla_heads = (mla_w_uq.shape[2] - mla_w_uk.shape[1]) // rope_dim
    nope_dim = mla_w_uk.shape[1] // mla_heads
    assert nope_dim == LANES and mla_w_uv.shape[1] == mla_heads * LANES and rope_dim <= LANES
    assert gla_dk_h % LANES == 0 and gla_dv_h % LANES == 0
    attn_scale = float(nope_dim + rope_dim) ** -0.5 * LOG2_E

    h = x.reshape(t, d)
    pos = positions.reshape(t, 1)
    row = lambda a: a.reshape(1, -1)

    cos_t = sin_t = k_cat = v_all = None
    for layer in range(depth):
        if layer < n_gla:
            a = layer
            w_main = _cast_cols(gla_w_in, a, n_main)
            w_up = jnp.pad(gla_w_gate_up[a], ((0, LANES - gla_w_gate_up.shape[1]), (0, 0))).astype(BF16)
            qkvr, log_a = _gla_inproj(h, row(norm_mix[layer]), w_main, gla_w_in, a, w_up, row(gla_b_gate[a]))
            o = _gla(qkvr, log_a, row(gla_norm[a]), batch=batch, seq=seq, heads=gla_heads,
                     dk_h=gla_dk_h, dv_h=gla_dv_h)
            h = _outproj(h, o, gla_w_out[a].astype(BF16))
        else:
            b = layer - n_gla
            if layer == n_gla:
                cos_t, sin_t = _rope_tables(pos, rope_dim)
                w_dkv = _mla_kv_weights(mla_w_dkv, lora=kv_lora, rope=rope_dim)
                k_cat, v_all = _mla_kv(h, row(kv_norm_in), w_dkv, row(mla_kv_norm),
                                       mla_w_uk.astype(BF16), mla_w_uv.astype(BF16), cos_t, sin_t,
                                       heads=mla_heads)
            w_n, w_rp, w_sp = _mla_q_weights(mla_w_uq[b], heads=mla_heads, nope=nope_dim, rope=rope_dim)
            q_cat = _mla_q(h, row(norm_mix[layer]), mla_w_dq[b].astype(BF16), row(mla_q_norm[b]),
                           w_n, w_rp, w_sp, cos_t, sin_t,
                           heads=mla_heads, scale=attn_scale)
            o = _attention(q_cat, k_cat, v_all, batch=batch, seq=seq, heads=mla_heads)
            h = _outproj(h, o, mla_w_o[b].astype(BF16))
        h = _mlp(h, row(norm_mlp[layer]), mlp_w1[layer].astype(BF16), mlp_w2[layer].astype(BF16),
                 row(final_norm), final_norm=(layer == depth - 1))
    return h.reshape(batch, seq, d)
```
